```python
import jax, jax.numpy as jnp
from jax import lax
import numpy as np

D_MODEL = 1024
BATCH = 8
SEQ = 2048
DEPTH = 2
DEC_BATCH = 128
DEC_SEQ = 1
PAST_LEN = 16384
PAGE_SIZE = 128

N_A_LAYERS = (DEPTH + 1) // 2
N_C_LAYERS = DEPTH // 2

A_HEADS = 4
A_DK = 128
A_DV = 128
A_KW = A_HEADS * A_DK
A_WIDTH = A_HEADS * A_DV
A_CHUNK = 32
B_WIDTH = D_MODEL - A_WIDTH
SC_WIDTH = 3
EVEN_SPLITS = (A_KW, 2 * A_KW, 2 * A_KW + A_WIDTH, 2 * A_KW + 2 * A_WIDTH,
               2 * A_KW + 2 * A_WIDTH + B_WIDTH, 2 * A_KW + 2 * A_WIDTH + 2 * B_WIDTH)
IN_EVEN = 2 * A_KW + 2 * A_WIDTH + 3 * B_WIDTH

M_INNER = 2 * D_MODEL
M_HEADDIM = 64
M_HEADS = M_INNER // M_HEADDIM
M_STATE = 128
M_GROUPS = 4
M_HPG = M_HEADS // M_GROUPS
M_CONV = 4
M_CHUNK = 64
M_GN = M_GROUPS * M_STATE
M_CONV_DIM = M_INNER + 2 * M_GN
IN_ODD = M_INNER + M_CONV_DIM + M_HEADS

D_FF = 4 * D_MODEL
EPS = 1e-6

kernel_name = 'hybrid_hgrn2_shortconv_ssd_step'


def rmsnorm(x, g):
    xf = x.astype(jnp.float32)
    y = xf * lax.rsqrt(jnp.mean(xf * xf, axis=-1, keepdims=True) + EPS)
    return (y * g.astype(jnp.float32)).astype(x.dtype)


def causal_dwconv(u, buf, w):
    full = jnp.concatenate([buf.astype(u.dtype), u], axis=1)
    T = u.shape[1]
    W = w.shape[0]
    out = sum(full[:, k:k + T] * w[k] for k in range(W))
    return out, full[:, full.shape[1] - (W - 1):]


def gla_chunked(q, k, v, logf, chunk):
    Bn, T, H, K = q.shape
    V = v.shape[-1]
    n = T // chunk
    r = lambda a: a.reshape(Bn, n, chunk, H, a.shape[-1]).astype(jnp.float32)
    q, k, v, logf = r(q), r(k), r(v), r(logf)
    b = jnp.cumsum(logf, axis=2)
    b_last = b[:, :, -1:]
    qd = q * jnp.exp(b)
    kd = k * jnp.exp(-b)
    mask = jnp.tril(jnp.ones((chunk, chunk), bool))
    att = jnp.where(mask, jnp.einsum('bnihk,bnjhk->bnhij', qd, kd), 0.0)
    o_intra = jnp.einsum('bnhij,bnjhv->bnihv', att, v)
    dS = jnp.einsum('bnjhk,bnjhv->bnhkv', k * jnp.exp(b_last - b), v)
    decay = jnp.exp(b_last[:, :, 0])

    def step(S, inp):
        d, ds = inp
        return d[..., None] * S + ds, S

    S_fin, S_in = lax.scan(step, jnp.zeros((Bn, H, K, V), jnp.float32),
                           (jnp.moveaxis(decay, 1, 0), jnp.moveaxis(dS, 1, 0)))
    S_in = jnp.moveaxis(S_in, 0, 1)
    o_inter = jnp.einsum('bnihk,bnhkv->bnihv', qd, S_in)
    return (o_intra + o_inter).reshape(Bn, T, H, V), S_fin


def gla_recurrent(q, k, v, logf, S0):
    def step(S, inp):
        qt, kt, vt, lft = inp
        S = jnp.exp(lft)[..., None] * S + kt[..., None] * vt[..., None, :]
        return S, jnp.einsum('bhk,bhkv->bhv', qt, S)

    xs = tuple(jnp.moveaxis(a.astype(jnp.float32), 1, 0) for a in (q, k, v, logf))
    S, o = lax.scan(step, S0.astype(jnp.float32), xs)
    return jnp.moveaxis(o, 0, 1), S


def ssd_chunked(x, dt, A, Bm, Cm, chunk):
    Bn, T, G, R, P = x.shape
    N = Bm.shape[-1]
    n = T // chunk
    x = x.reshape(Bn, n, chunk, G, R, P)
    dt = dt.reshape(Bn, n, chunk, G, R)
    Bm = Bm.reshape(Bn, n, chunk, G, N)
    Cm = Cm.reshape(Bn, n, chunk, G, N)
    cs = jnp.cumsum(dt * A, axis=2)
    seg = cs[:, :, :, None] - cs[:, :, None, :]
    mask = jnp.tril(jnp.ones((chunk, chunk), bool))[:, :, None, None]
    decay_ij = jnp.exp(jnp.where(mask, seg, -jnp.inf))
    cb = jnp.einsum('bnigs,bnjgs->bnijg', Cm, Bm)
    wts = cb[..., None] * decay_ij * dt[:, :, None]
    y_diag = jnp.einsum('bnijgr,bnjgrp->bnigrp', wts, x)
    cs_last = cs[:, :, -1:]
    xw = x * (jnp.exp(cs_last - cs) * dt)[..., None]
    dS = jnp.einsum('bnjgs,bnjgrp->bngrps', Bm, xw)
    chunk_decay = jnp.exp(cs_last[:, :, 0])

    def step(S, inp):
        d, ds = inp
        return d[..., None, None] * S + ds, S

    S_fin, S_in = lax.scan(step, jnp.zeros((Bn, G, R, P, N), jnp.float32),
                           (jnp.moveaxis(chunk_decay, 1, 0), jnp.moveaxis(dS, 1, 0)))
    S_in = jnp.moveaxis(S_in, 0, 1)
    y_off = jnp.einsum('bnigs,bngrps->bnigrp', Cm, S_in) * jnp.exp(cs)[..., None]
    return (y_diag + y_off).reshape(Bn, T, G, R, P), S_fin


def ssd_recurrent(x, dt, A, Bm, Cm, S0):
    def step(S, inp):
        xt, dtt, Bt, Ct = inp
        S = (jnp.exp(dtt * A)[..., None, None] * S
             + (dtt[..., None] * xt)[..., None] * Bt[:, :, None, None, :])
        return S, jnp.einsum('bgs,bgrps->bgrp', Ct, S)

    xs = tuple(jnp.moveaxis(a, 1, 0) for a in (x, dt, Bm, Cm))
    S, y = lax.scan(step, S0, xs)
    return jnp.moveaxis(y, 0, 1), S


def even_mixer(h, sc_buf, hgrn_state, w_in, lb, gnorm_w, sc_w, w_out, prompt):
    Bn, T, _ = h.shape
    proj = h @ w_in
    q, fz, iv, go, bg, cg, hv = jnp.split(proj, EVEN_SPLITS, axis=-1)
    f = lb + (1.0 - lb) * jax.nn.sigmoid(fz.astype(jnp.float32))
    hs = lambda a, d: a.reshape(Bn, T, A_HEADS, d)
    qh = hs(q.astype(jnp.float32), A_DK)
    kh = hs(1.0 - f, A_DK)
    lf = hs(jnp.log(f), A_DK)
    vh = hs(iv.astype(jnp.float32), A_DV)
    if prompt:
        o, S = gla_chunked(qh, kh, vh, lf, A_CHUNK)
    else:
        o, S = gla_recurrent(qh, kh, vh, lf, hgrn_state)
    o = rmsnorm(o, gnorm_w) * jax.nn.silu(hs(go.astype(jnp.float32), A_DV))
    o_a = o.reshape(Bn, T, A_WIDTH).astype(h.dtype)
    if sc_buf is None:
        sc_buf = jnp.zeros((Bn, SC_WIDTH - 1, B_WIDTH), h.dtype)
    conv, new_sc = causal_dwconv(cg * hv, sc_buf, sc_w)
    o_b = bg * conv
    out = jnp.concatenate([o_a, o_b], axis=-1) @ w_out
    return out, S, new_sc


def mamba_mixer(h, conv_buf, ssm_state, w_in, conv_w, conv_b, dt_bias, a_log, d_skip, norm_w, w_out, prompt):
    Bn, T, _ = h.shape
    z, xbc, dt_raw = jnp.split(h @ w_in, (M_INNER, M_INNER + M_CONV_DIM), axis=-1)
    if conv_buf is None:
        conv_buf = jnp.zeros((Bn, M_CONV - 1, M_CONV_DIM), h.dtype)
    xbc, new_buf = causal_dwconv(xbc, conv_buf, conv_w)
    xbc = jax.nn.silu(xbc + conv_b)
    xs, Bm, Cm = jnp.split(xbc.astype(jnp.float32), (M_INNER, M_INNER + M_GN), axis=-1)
    xs = xs.reshape(Bn, T, M_GROUPS, M_HPG, M_HEADDIM)
    Bm = Bm.reshape(Bn, T, M_GROUPS, M_STATE)
    Cm = Cm.reshape(Bn, T, M_GROUPS, M_STATE)
    dt = jax.nn.softplus(dt_raw.astype(jnp.float32) + dt_bias.astype(jnp.float32))
    dt = dt.reshape(Bn, T, M_GROUPS, M_HPG)
    A = -jnp.exp(a_log.astype(jnp.float32)).reshape(M_GROUPS, M_HPG)
    if prompt:
        y, S = ssd_chunked(xs, dt, A, Bm, Cm, M_CHUNK)
    else:
        S0 = ssm_state.astype(jnp.float32).reshape(Bn, M_GROUPS, M_HPG, M_HEADDIM, M_STATE)
        y, S = ssd_recurrent(xs, dt, A, Bm, Cm, S0)
    y = y + d_skip.astype(jnp.float32).reshape(M_GROUPS, M_HPG)[..., None] * xs
    y = y.reshape(Bn, T, M_INNER) * jax.nn.silu(z.astype(jnp.float32))
    y = rmsnorm(y.reshape(Bn, T, M_GROUPS, M_INNER // M_GROUPS),
                norm_w.reshape(M_GROUPS, M_INNER // M_GROUPS)).reshape(Bn, T, M_INNER)
    out = y.astype(h.dtype) @ w_out
    return out, S.reshape(Bn, M_HEADS, M_HEADDIM, M_STATE), new_buf


def trunk(x, c, st_hgrn, st_sc, st_ssm, st_mconv, params, prompt):
    (ada_w, ada_b, norm_mix, norm_mlp, norm_final, w_in_even, hgrn_lb, hgrn_gnorm, sc_w, w_out_even,
     w_in_odd, mconv_w, mconv_b, dt_bias, a_log, d_skip, m_norm, w_out_odd, mlp_w1, mlp_w2) = params
    p_lb = jax.nn.softmax(hgrn_lb.astype(jnp.float32), axis=0)
    lbs = jnp.cumsum(p_lb, axis=0) - p_lb[0]
    hg_l, sc_l, ssm_l, mc_l = [], [], [], []
    for l in range(DEPTH):
        j = l // 2
        mod = jax.nn.silu(c) @ ada_w[l] + ada_b[l]
        sh1, s1, g1, sh2, s2, g2 = jnp.split(mod[:, None, :], 6, axis=-1)
        h = rmsnorm(x, norm_mix[l]) * (1.0 + s1) + sh1
        if l % 2 == 0:
            out, S, buf = even_mixer(h, None if prompt else st_sc[j], None if prompt else st_hgrn[j],
                                     w_in_even[j], lbs[j + 1], hgrn_gnorm[j], sc_w[j], w_out_even[j], prompt)
            hg_l.append(S.astype(x.dtype))
            sc_l.append(buf.astype(x.dtype))
        else:
            out, S, buf = mamba_mixer(h, None if prompt else st_mconv[j], None if prompt else st_ssm[j],
                                      w_in_odd[j], mconv_w[j], mconv_b[j], dt_bias[j], a_log[j], d_skip[j],
                                      m_norm[j], w_out_odd[j], prompt)
            ssm_l.append(S.astype(x.dtype))
            mc_l.append(buf.astype(x.dtype))
        x = x + g1 * out
        h = rmsnorm(x, norm_mlp[l]) * (1.0 + s2) + sh2
        x = x + g2 * (jnp.square(jax.nn.relu(h @ mlp_w1[l])) @ mlp_w2[l])
    y = rmsnorm(x, norm_final)
    return y, jnp.stack(hg_l), jnp.stack(sc_l), jnp.stack(ssm_l), jnp.stack(mc_l)


def setup_inputs(seed: int = 0) -> dict:
    key = jax.random.key(seed)
    ks = jax.random.split(key, 32)
    nrm = lambda k, shape, s: jax.random.normal(k, shape, jnp.float32) * s
    dt0 = jnp.exp(jax.random.uniform(ks[20], (N_C_LAYERS, M_HEADS), jnp.float32)
                  * (np.log(0.1) - np.log(0.001)) + np.log(0.001))
    return {
        'x_prompt': nrm(ks[0], (BATCH, SEQ, D_MODEL), 1.0),
        'x_sample': nrm(ks[1], (DEC_BATCH, DEC_SEQ, D_MODEL), 1.0),
        'c_prompt': nrm(ks[2], (BATCH, D_MODEL), 1.0),
        'c_sample': nrm(ks[3], (DEC_BATCH, D_MODEL), 1.0),
        'state_hgrn': nrm(ks[4], (N_A_LAYERS, DEC_BATCH, A_HEADS, A_DK, A_DV), 0.3),
        'state_shortconv': nrm(ks[5], (N_A_LAYERS, DEC_BATCH, SC_WIDTH - 1, B_WIDTH), 1.0),
        'state_ssm': nrm(ks[6], (N_C_LAYERS, DEC_BATCH, M_HEADS, M_HEADDIM, M_STATE), 0.3),
        'state_mconv': nrm(ks[7], (N_C_LAYERS, DEC_BATCH, M_CONV - 1, M_CONV_DIM), 1.0),
        'ada_w': nrm(ks[8], (DEPTH, D_MODEL, 6 * D_MODEL), 0.5 * D_MODEL ** -0.5),
        'ada_b': nrm(ks[9], (DEPTH, 6 * D_MODEL), 0.02),
        'norm_mix': 1.0 + nrm(ks[10], (DEPTH, D_MODEL), 0.02),
        'norm_mlp': 1.0 + nrm(ks[11], (DEPTH, D_MODEL), 0.02),
        'norm_final': 1.0 + nrm(ks[12], (D_MODEL,), 0.02),
        'w_in_even': nrm(ks[13], (N_A_LAYERS, D_MODEL, IN_EVEN), D_MODEL ** -0.5),
        'hgrn_lb': nrm(ks[14], (N_A_LAYERS + 1, A_KW), 0.1),
        'hgrn_gnorm': 1.0 + nrm(ks[15], (N_A_LAYERS, A_DV), 0.02),
        'sc_w': nrm(ks[16], (N_A_LAYERS, SC_WIDTH, B_WIDTH), SC_WIDTH ** -0.5),
        'w_out_even': nrm(ks[17], (N_A_LAYERS, D_MODEL, D_MODEL), D_MODEL ** -0.5),
        'w_in_odd': nrm(ks[18], (N_C_LAYERS, D_MODEL, IN_ODD), D_MODEL ** -0.5),
        'mconv_w': nrm(ks[19], (N_C_LAYERS, M_CONV, M_CONV_DIM), M_CONV ** -0.5),
        'mconv_b': nrm(ks[21], (N_C_LAYERS, M_CONV_DIM), 0.02),
        'dt_bias': dt0 + jnp.log(-jnp.expm1(-dt0)),
        'a_log': jnp.log(jax.random.uniform(ks[22], (N_C_LAYERS, M_HEADS), jnp.float32, 1.0, 16.0)),
        'd_skip': 1.0 + nrm(ks[23], (N_C_LAYERS, M_HEADS), 0.1),
        'm_norm': 1.0 + nrm(ks[24], (N_C_LAYERS, M_INNER), 0.02),
        'w_out_odd': nrm(ks[25], (N_C_LAYERS, M_INNER, D_MODEL), M_INNER ** -0.5),
        'mlp_w1': nrm(ks[26], (DEPTH, D_MODEL, D_FF), D_MODEL ** -0.5),
        'mlp_w2': nrm(ks[27], (DEPTH, D_FF, D_MODEL), D_FF ** -0.5),
    }


def reference(x_prompt, x_sample, c_prompt, c_sample, state_hgrn, state_shortconv, state_ssm, state_mconv,
              ada_w, ada_b, norm_mix, norm_mlp, norm_final, w_in_even, hgrn_lb, hgrn_gnorm, sc_w, w_out_even,
              w_in_odd, mconv_w, mconv_b, dt_bias, a_log, d_skip, m_norm, w_out_odd, mlp_w1, mlp_w2):
    params = (ada_w, ada_b, norm_mix, norm_mlp, norm_final, w_in_even, hgrn_lb, hgrn_gnorm, sc_w, w_out_even,
              w_in_odd, mconv_w, mconv_b, dt_bias, a_log, d_skip, m_norm, w_out_odd, mlp_w1, mlp_w2)
    y_prompt, hg_p, sc_p, ssm_p, mc_p = trunk(x_prompt, c_prompt, None, None, None, None, params, True)
    y_sample, hg_s, sc_s, ssm_s, mc_s = trunk(x_sample, c_sample, state_hgrn, state_shortconv,
                                              state_ssm, state_mconv, params, False)
    return (y_prompt, y_sample, hg_p, hg_s, sc_p, sc_s, ssm_p, ssm_s, mc_p, mc_s)
```

```python
import functools

import jax
import jax.numpy as jnp
from jax import lax
from jax.experimental import pallas as pl
from jax.experimental.pallas import tpu as pltpu

F32 = jnp.float32
BF16 = jnp.bfloat16
EPS = 1e-6

D_MODEL = 1024
A_HEADS = 4
A_DK = 128
A_WIDTH = 512
B_WIDTH = 512
M_INNER = 2048
M_HEADDIM = 64
M_HEADS = 32
M_STATE = 128
M_GROUPS = 4
M_GN = M_GROUPS * M_STATE
M_CONV_DIM = M_INNER + 2 * M_GN
M_CONV = 4
SC_WIDTH = 3
D_FF = 4096

LANES = 128
GLA_CHUNK = 32
SSD_CHUNK = 128
V7X_VMEM_BYTES = 64 * 1024 * 1024
VMEM_LIMIT = V7X_VMEM_BYTES - 8 * 1024 * 1024


def _cparams(*sem):
    return pltpu.CompilerParams(dimension_semantics=sem, vmem_limit_bytes=VMEM_LIMIT)


def _rms(x):
    return x * lax.rsqrt(jnp.mean(x * x, axis=-1, keepdims=True) + EPS)


def _silu(x):
    return x * jax.nn.sigmoid(x)


def _softplus(x):
    return jnp.maximum(x, 0.0) + jnp.log1p(jnp.exp(-jnp.abs(x)))


def _bdot(a, b):
    return jnp.dot(a.astype(BF16), b.astype(BF16), preferred_element_type=F32)


def _bdot_nt(a, b):
    return lax.dot_general(a.astype(BF16), b.astype(BF16), (((1,), (1,)), ((), ())),
                           preferred_element_type=F32)


def _bdot_tn(a, b):
    return lax.dot_general(a.astype(BF16), b.astype(BF16), (((0,), (0,)), ((), ())),
                           preferred_element_type=F32)


def _dot3(m01, x):
    hi = x.astype(BF16)
    r = x - hi.astype(F32)
    mid = r.astype(BF16)
    lo = (r - mid.astype(F32)).astype(BF16)
    d = lambda v: jnp.dot(m01, v, preferred_element_type=F32)
    return d(hi) + d(mid) + d(lo)


def _tri_mask(n, seg):
    r = lax.broadcasted_iota(jnp.int32, (n, n), 0)
    c = lax.broadcasted_iota(jnp.int32, (n, n), 1)
    shift = seg.bit_length() - 1
    same = (r >> shift) == (c >> shift)
    return jnp.where(same, jnp.where(c <= r, 1.0, 0.0), 0.0).astype(BF16)


def _pad_t(a):
    r = a.shape[0]
    return jnp.concatenate([a, jnp.zeros((LANES - r, LANES), F32)], axis=0).T


def _hgrn_lower_bound(lb_all, slot):
    m = jnp.max(lb_all, axis=0, keepdims=True)
    e = jnp.exp(lb_all - m)
    p = e / jnp.sum(e, axis=0, keepdims=True)
    return jnp.sum(p[1:slot + 1], axis=0, keepdims=True)


def _mod_kernel(c_ref, w_ref, b_ref, o_ref):
    c = c_ref[...]
    a = _silu(c).astype(BF16)
    o_ref[0] = jnp.dot(a, w_ref[0].astype(BF16), preferred_element_type=F32) + b_ref[0]


def _adaln_mod(c_all, ada_w, ada_b):
    n_layers, d, n = ada_w.shape
    r = c_all.shape[0]
    tn = 1536
    return pl.pallas_call(
        _mod_kernel,
        grid=(n_layers, n // tn),
        in_specs=[pl.BlockSpec((r, d), lambda l, j: (0, 0)),
                  pl.BlockSpec((1, d, tn), lambda l, j: (l, 0, j)),
                  pl.BlockSpec((1, 1, tn), lambda l, j: (l, 0, j))],
        out_specs=pl.BlockSpec((1, r, tn), lambda l, j: (l, 0, j)),
        out_shape=jax.ShapeDtypeStruct((n_layers, r, n), F32),
        compiler_params=_cparams("parallel", "parallel"),
        name="adaln_mod",
    )(c_all, ada_w, ada_b.reshape(n_layers, 1, n))


def _mod_spec(mod, tm, col):
    if mod.shape[1] == 1:
        return pl.BlockSpec((1, 1, D_MODEL), lambda b, t: (b, 0, col))
    return pl.BlockSpec((1, tm, D_MODEL), lambda b, t: (b, t, col))


def _norm_proj_kernel(x_ref, sh_ref, sc_ref, g_ref, w_ref, *rest, has_dt):
    x = x_ref[0]
    h = (_rms(x) * g_ref[...]) * (1.0 + sc_ref[0]) + sh_ref[0]
    if has_dt:
        wdt_ref, o_ref, odt_ref = rest
        odt_ref[0] = jnp.dot(h, wdt_ref[...], preferred_element_type=F32,
                             precision=lax.Precision.HIGHEST)
    else:
        (o_ref,) = rest
    o_ref[0] = jnp.dot(h.astype(BF16), w_ref[...], preferred_element_type=F32)


def _norm_proj(x, mod, g, w, w_dt=None):
    bn, t, d = x.shape
    tm = min(t, 512)
    n = w.shape[1]
    has_dt = w_dt is not None
    in_specs = [pl.BlockSpec((1, tm, d), lambda b, i: (b, i, 0)),
                _mod_spec(mod, tm, 0), _mod_spec(mod, tm, 1),
                pl.BlockSpec((1, d), lambda b, i: (0, 0)),
                pl.BlockSpec((d, n), lambda b, i: (0, 0), pipeline_mode=pl.Buffered(1))]
    out_specs = [pl.BlockSpec((1, tm, n), lambda b, i: (b, i, 0))]
    out_shape = [jax.ShapeDtypeStruct((bn, t, n), F32)]
    args = [x, mod, mod, g.reshape(1, d), w]
    if has_dt:
        in_specs.append(pl.BlockSpec((d, LANES), lambda b, i: (0, 0)))
        out_specs.append(pl.BlockSpec((1, tm, LANES), lambda b, i: (b, i, 0)))
        out_shape.append(jax.ShapeDtypeStruct((bn, t, LANES), F32))
        args.append(w_dt)
    outs = pl.pallas_call(
        functools.partial(_norm_proj_kernel, has_dt=has_dt),
        grid=(bn, t // tm),
        in_specs=in_specs, out_specs=out_specs, out_shape=out_shape,
        compiler_params=_cparams("parallel", "parallel"),
        name="norm_proj_dt" if has_dt else "norm_proj",
    )(*args)
    return outs


def _post_kernel(x_ref, o_ref, g1_ref, sh2_ref, s2_ref, g2_ref, nm_ref, wo_ref, w1_ref, w2_ref,
                 *rest, final, fc):
    x = x_ref[0]
    x1 = x + g1_ref[0] * jnp.dot(o_ref[0].astype(BF16), wo_ref[...], preferred_element_type=F32)
    h = ((_rms(x1) * nm_ref[...]) * (1.0 + s2_ref[0]) + sh2_ref[0]).astype(BF16)
    acc = jnp.zeros_like(x1)
    for c in range(D_FF // fc):
        a = jnp.dot(h, w1_ref[:, c * fc:(c + 1) * fc], preferred_element_type=F32)
        a = jnp.square(jnp.maximum(a, 0.0)).astype(BF16)
        acc = acc + jnp.dot(a, w2_ref[c * fc:(c + 1) * fc, :], preferred_element_type=F32)
    x2 = x1 + g2_ref[0] * acc
    if final:
        nf_ref, out_ref = rest
        out_ref[0] = _rms(x2) * nf_ref[...]
    else:
        (out_ref,) = rest
        out_ref[0] = x2


def _post(x, o, mod, nm, wo, w1, w2, nf=None):
    bn, t, d = x.shape
    tm = min(t, 512)
    kin = o.shape[2]
    final = nf is not None
    const = lambda shape: pl.BlockSpec(shape, lambda b, i: (0,) * len(shape), pipeline_mode=pl.Buffered(1))
    in_specs = [pl.BlockSpec((1, tm, d), lambda b, i: (b, i, 0)),
                pl.BlockSpec((1, tm, kin), lambda b, i: (b, i, 0)),
                _mod_spec(mod, tm, 2), _mod_spec(mod, tm, 3), _mod_spec(mod, tm, 4), _mod_spec(mod, tm, 5),
                pl.BlockSpec((1, d), lambda b, i: (0, 0)),
                const((kin, d)), const((d, D_FF)), const((D_FF, d))]
    args = [x, o, mod, mod, mod, mod, nm.reshape(1, d), wo, w1, w2]
    if final:
        in_specs.append(pl.BlockSpec((1, d), lambda b, i: (0, 0)))
        args.append(nf.reshape(1, d))
    return pl.pallas_call(
        functools.partial(_post_kernel, final=final, fc=1024),
        grid=(bn, t // tm),
        in_specs=in_specs,
        out_specs=pl.BlockSpec((1, tm, d), lambda b, i: (b, i, 0)),
        out_shape=jax.ShapeDtypeStruct((bn, t, d), F32),
        compiler_params=_cparams("parallel", "parallel"),
        name="post_final" if final else "post",
    )(*args)


def _even_prompt_kernel(proj_ref, lb_ref, gn_ref, scw_ref, o_ref, s_ref, sc_ref, st_scr, ubuf,
                        *, tb, slot):
    t = pl.program_id(1)

    @pl.when(t == 0)
    def _():
        st_scr[...] = jnp.zeros_like(st_scr)
        ubuf[0:8, :] = jnp.zeros((8, B_WIDTH), F32)

    lb = _hgrn_lower_bound(lb_ref[...], slot)
    tri = _tri_mask(tb, GLA_CHUNK)
    r = lax.broadcasted_iota(jnp.int32, (GLA_CHUNK, GLA_CHUNK), 0)
    c = lax.broadcasted_iota(jnp.int32, (GLA_CHUNK, GLA_CHUNK), 1)
    causal = c <= r
    gn = gn_ref[...]
    n_chunks = tb // GLA_CHUNK
    for h in range(A_HEADS):
        cols = lambda part: slice(part * A_WIDTH + h * A_DK, part * A_WIDTH + (h + 1) * A_DK)
        q = proj_ref[0, :, cols(0)]
        fz = proj_ref[0, :, cols(1)]
        v = proj_ref[0, :, cols(2)]
        go = proj_ref[0, :, cols(3)]
        lbh = lb[:, h * A_DK:(h + 1) * A_DK]
        f = lbh + (1.0 - lbh) * jax.nn.sigmoid(fz)
        k = 1.0 - f
        b = _dot3(tri, jnp.log(f))
        st = st_scr[h]
        for n in range(n_chunks):
            rows = slice(n * GLA_CHUNK, (n + 1) * GLA_CHUNK)
            bn = b[rows]
            bl = b[(n + 1) * GLA_CHUNK - 1:(n + 1) * GLA_CHUNK]
            qd = q[rows] * jnp.exp(bn)
            kd = k[rows] * jnp.exp(-bn)
            kdec = k[rows] * jnp.exp(bl - bn)
            att = jnp.where(causal, _bdot_nt(qd, kd), 0.0)
            o = _bdot(att, v[rows]) + _bdot_nt(qd, st)
            st = st * jnp.exp(bl) + _bdot_tn(v[rows], kdec)
            o = (_rms(o) * gn) * _silu(go[rows])
            o_ref[0, rows, h * A_DK:(h + 1) * A_DK] = o.astype(BF16)
        st_scr[h] = st

    off = 4 * A_WIDTH
    bg = proj_ref[0, :, off:off + B_WIDTH]
    cg = proj_ref[0, :, off + B_WIDTH:off + 2 * B_WIDTH]
    hv = proj_ref[0, :, off + 2 * B_WIDTH:off + 3 * B_WIDTH]
    u = cg * hv
    ubuf[8:8 + tb, :] = u
    w = scw_ref[...]
    conv = ubuf[6:6 + tb, :] * w[0:1] + ubuf[7:7 + tb, :] * w[1:2] + u * w[2:3]
    o_ref[0, :, A_WIDTH:A_WIDTH + B_WIDTH] = (bg * conv).astype(BF16)
    ubuf[6:8, :] = u[tb - 2:tb]

    @pl.when(t == pl.num_programs(1) - 1)
    def _():
        for h in range(A_HEADS):
            s_ref[0, h] = st_scr[h].T
        sc_ref[0] = u[tb - 2:tb]


def _even_prompt(proj, hgrn_lb, gnorm, sc_w, slot):
    bn, t, n = proj.shape
    tb = 256
    return pl.pallas_call(
        functools.partial(_even_prompt_kernel, tb=tb, slot=slot),
        grid=(bn, t // tb),
        in_specs=[pl.BlockSpec((1, tb, n), lambda b, i: (b, i, 0)),
                  pl.BlockSpec(hgrn_lb.shape, lambda b, i: (0, 0)),
                  pl.BlockSpec((1, A_DK), lambda b, i: (0, 0)),
                  pl.BlockSpec((SC_WIDTH, B_WIDTH), lambda b, i: (0, 0))],
        out_specs=[pl.BlockSpec((1, tb, D_MODEL), lambda b, i: (b, i, 0)),
                   pl.BlockSpec((1, A_HEADS, A_DK, A_DK), lambda b, i: (b, 0, 0, 0)),
                   pl.BlockSpec((1, SC_WIDTH - 1, B_WIDTH), lambda b, i: (b, 0, 0))],
        out_shape=[jax.ShapeDtypeStruct((bn, t, D_MODEL), BF16),
                   jax.ShapeDtypeStruct((bn, A_HEADS, A_DK, A_DK), F32),
                   jax.ShapeDtypeStruct((bn, SC_WIDTH - 1, B_WIDTH), F32)],
        scratch_shapes=[pltpu.VMEM((A_HEADS, A_DK, A_DK), F32),
                        pltpu.VMEM((tb + 8, B_WIDTH), F32)],
        compiler_params=_cparams("parallel", "arbitrary"),
        name="even_prompt",
    )(proj, hgrn_lb, gnorm.reshape(1, A_DK), sc_w)


def _even_sample_kernel(proj_ref, s_ref, scb_ref, lb_ref, gn_ref, scw_ref, o_ref, so_ref, sco_ref,
                        *, tt, slot):
    lb = _hgrn_lower_bound(lb_ref[...], slot)
    gn = gn_ref[...]
    for h in range(A_HEADS):
        cols = lambda part: slice(part * A_WIDTH + h * A_DK, part * A_WIDTH + (h + 1) * A_DK)
        q = proj_ref[:, cols(0)]
        fz = proj_ref[:, cols(1)]
        v = proj_ref[:, cols(2)]
        go = proj_ref[:, cols(3)]
        lbh = lb[:, h * A_DK:(h + 1) * A_DK]
        f = lbh + (1.0 - lbh) * jax.nn.sigmoid(fz)
        f_t = _pad_t(f)
        q_t = _pad_t(q)
        outs = []
        for i in range(tt):
            fc = f_t[:, i:i + 1]
            s_new = fc * s_ref[i, h] + (1.0 - fc) * v[i:i + 1, :]
            so_ref[i, h] = s_new
            outs.append(jnp.sum(q_t[:, i:i + 1] * s_new, axis=0, keepdims=True))
        o = jnp.concatenate(outs, axis=0)
        o_ref[:, h * A_DK:(h + 1) * A_DK] = (_rms(o) * gn) * _silu(go)

    off = 4 * A_WIDTH
    bg = proj_ref[:, off:off + B_WIDTH]
    u = proj_ref[:, off + B_WIDTH:off + 2 * B_WIDTH] * proj_ref[:, off + 2 * B_WIDTH:off + 3 * B_WIDTH]
    w = scw_ref[...]
    conv = scb_ref[:, 0:B_WIDTH] * w[0:1] + scb_ref[:, B_WIDTH:2 * B_WIDTH] * w[1:2] + u * w[2:3]
    o_ref[:, A_WIDTH:A_WIDTH + B_WIDTH] = bg * conv
    sco_ref[:, 0:B_WIDTH] = scb_ref[:, B_WIDTH:2 * B_WIDTH]
    sco_ref[:, B_WIDTH:2 * B_WIDTH] = u


def _even_sample(proj, state, sc_buf, hgrn_lb, gnorm, sc_w, slot):
    bs, n = proj.shape
    tt = 8
    nb = sc_buf.shape[1]
    return pl.pallas_call(
        functools.partial(_even_sample_kernel, tt=tt, slot=slot),
        grid=(bs // tt,),
        in_specs=[pl.BlockSpec((tt, n), lambda i: (i, 0)),
                  pl.BlockSpec((tt, A_HEADS, A_DK, A_DK), lambda i: (i, 0, 0, 0)),
                  pl.BlockSpec((tt, nb), lambda i: (i, 0)),
                  pl.BlockSpec(hgrn_lb.shape, lambda i: (0, 0)),
                  pl.BlockSpec((1, A_DK), lambda i: (0, 0)),
                  pl.BlockSpec((SC_WIDTH, B_WIDTH), lambda i: (0, 0))],
        out_specs=[pl.BlockSpec((tt, D_MODEL), lambda i: (i, 0)),
                   pl.BlockSpec((tt, A_HEADS, A_DK, A_DK), lambda i: (i, 0, 0, 0)),
                   pl.BlockSpec((tt, nb), lambda i: (i, 0))],
        out_shape=[jax.ShapeDtypeStruct((bs, D_MODEL), F32),
                   jax.ShapeDtypeStruct(state.shape, F32),
                   jax.ShapeDtypeStruct(sc_buf.shape, F32)],
        compiler_params=_cparams("parallel"),
        name="even_sample",
    )(proj, state, sc_buf, hgrn_lb, gnorm.reshape(1, A_DK), sc_w)


def _pair_cols(m, lo, h0):
    return jnp.where(lo, m[:, h0:h0 + 1], m[:, h0 + 1:h0 + 2])


def _mamba_prompt_kernel(zx_ref, dtr_ref, cw_ref, cb_ref, dtb_ref, alog_ref, dsk_ref, nw_ref,
                         y_ref, s_ref, mc_ref, st_scr, cbuf):
    t = pl.program_id(1)
    q = SSD_CHUNK

    @pl.when(t == 0)
    def _():
        st_scr[...] = jnp.zeros_like(st_scr)
        cbuf[0:8, :] = jnp.zeros((8, M_CONV_DIM), F32)

    xr = zx_ref[0, :, M_INNER:M_INNER + M_CONV_DIM]
    cbuf[8:8 + q, :] = xr
    cw = cw_ref[...]
    conv = (cbuf[5:5 + q, :] * cw[0:1] + cbuf[6:6 + q, :] * cw[1:2]
            + cbuf[7:7 + q, :] * cw[2:3] + xr * cw[3:4])
    cbuf[5:8, :] = xr[q - 3:q]
    xbc = _silu(conv + cb_ref[...])

    dt = _softplus(dtr_ref[0] + dtb_ref[...])
    d_a = dt * (-jnp.exp(alog_ref[...]))
    cs = _dot3(_tri_mask(q, q), d_a)
    cs_t = cs.T
    dt_t = dt.T
    ecs = jnp.exp(cs)
    cs_last = cs[q - 1:q, :]
    wx = jnp.exp(cs_last - cs) * dt
    dec = jnp.exp(cs_last)

    r = lax.broadcasted_iota(jnp.int32, (q, q), 0)
    c = lax.broadcasted_iota(jnp.int32, (q, q), 1)
    causal = c <= r
    lo = c < M_HEADDIM
    lo1 = lo[0:1]
    gw = M_INNER // M_GROUPS
    for g in range(M_GROUPS):
        bm = xbc[:, M_INNER + g * M_STATE:M_INNER + (g + 1) * M_STATE]
        cm = xbc[:, M_INNER + M_GN + g * M_STATE:M_INNER + M_GN + (g + 1) * M_STATE]
        cb = _bdot_nt(cm, bm)
        st_g = st_scr[:, g * gw:(g + 1) * gw]
        y_off = _bdot(cm, st_g)
        ys, xws, decs = [], [], []
        for p in range(gw // LANES):
            m = g * (gw // LANES) + p
            h0 = 2 * m
            xp = xbc[:, m * LANES:(m + 1) * LANES]
            yd = []
            for h in (h0, h0 + 1):
                seg = jnp.minimum(cs[:, h:h + 1] - cs_t[h:h + 1, :], 0.0)
                w = cb * jnp.where(causal, jnp.exp(seg), 0.0) * dt_t[h:h + 1, :]
                yd.append(_bdot(w, xp))
            y = jnp.where(lo, yd[0], yd[1])
            y = y + y_off[:, p * LANES:(p + 1) * LANES] * _pair_cols(ecs, lo, h0)
            ys.append(y + dsk_ref[:, m * LANES:(m + 1) * LANES] * xp)
            xws.append(xp * _pair_cols(wx, lo, h0))
            decs.append(_pair_cols(dec, lo1, h0))
        xw = jnp.concatenate(xws, axis=1)
        st_scr[:, g * gw:(g + 1) * gw] = st_g * jnp.concatenate(decs, axis=1) + _bdot(bm.T, xw)
        y = jnp.concatenate(ys, axis=1) * _silu(zx_ref[0, :, g * gw:(g + 1) * gw])
        y_ref[0, :, g * gw:(g + 1) * gw] = (_rms(y) * nw_ref[:, g * gw:(g + 1) * gw]).astype(BF16)

    @pl.when(t == pl.num_programs(1) - 1)
    def _():
        for m in range(M_INNER // LANES):
            blk = st_scr[:, m * LANES:(m + 1) * LANES].T
            s_ref[0, 2 * m] = blk[0:M_HEADDIM]
            s_ref[0, 2 * m + 1] = blk[M_HEADDIM:2 * M_HEADDIM]
        mc_ref[0] = xr[q - 3:q]


def _mamba_prompt(zx, dtr, conv_w, conv_b, dtb, alog, dsk, nw):
    bn, t, n = zx.shape
    q = SSD_CHUNK
    row = lambda w: pl.BlockSpec((1, w), lambda b, i: (0, 0))
    return pl.pallas_call(
        _mamba_prompt_kernel,
        grid=(bn, t // q),
        in_specs=[pl.BlockSpec((1, q, n), lambda b, i: (b, i, 0)),
                  pl.BlockSpec((1, q, LANES), lambda b, i: (b, i, 0)),
                  pl.BlockSpec((M_CONV, M_CONV_DIM), lambda b, i: (0, 0)),
                  row(M_CONV_DIM), row(LANES), row(LANES), row(M_INNER), row(M_INNER)],
        out_specs=[pl.BlockSpec((1, q, M_INNER), lambda b, i: (b, i, 0)),
                   pl.BlockSpec((1, M_HEADS, M_HEADDIM, M_STATE), lambda b, i: (b, 0, 0, 0)),
                   pl.BlockSpec((1, M_CONV - 1, M_CONV_DIM), lambda b, i: (b, 0, 0))],
        out_shape=[jax.ShapeDtypeStruct((bn, t, M_INNER), BF16),
                   jax.ShapeDtypeStruct((bn, M_HEADS, M_HEADDIM, M_STATE), F32),
                   jax.ShapeDtypeStruct((bn, M_CONV - 1, M_CONV_DIM), F32)],
        scratch_shapes=[pltpu.VMEM((M_STATE, M_INNER), F32),
                        pltpu.VMEM((q + 8, M_CONV_DIM), F32)],
        compiler_params=_cparams("parallel", "arbitrary"),
        name="mamba_prompt",
    )(zx, dtr, conv_w, conv_b, dtb, alog, dsk, nw)


def _mconv_sample_kernel(zx_ref, dtr_ref, buf_ref, cw_ref, cb_ref, dtb_ref, alog_ref,
                         xbc_ref, nbuf_ref, dte_ref, dece_ref):
    n = M_CONV_DIM
    u = zx_ref[:, M_INNER:M_INNER + n]
    cw = cw_ref[...]
    conv = (buf_ref[:, 0:n] * cw[0:1] + buf_ref[:, n:2 * n] * cw[1:2]
            + buf_ref[:, 2 * n:3 * n] * cw[2:3] + u * cw[3:4])
    xbc_ref[...] = _silu(conv + cb_ref[...])
    nbuf_ref[:, 0:2 * n] = buf_ref[:, n:3 * n]
    nbuf_ref[:, 2 * n:3 * n] = u
    dt = _softplus(dtr_ref[...] + dtb_ref[...])
    dec = jnp.exp(dt * (-jnp.exp(alog_ref[...])))
    lo = lax.broadcasted_iota(jnp.int32, (dt.shape[0], LANES), 1) < M_HEADDIM
    for m in range(M_INNER // LANES):
        dte_ref[:, m * LANES:(m + 1) * LANES] = _pair_cols(dt, lo, 2 * m)
        dece_ref[:, m * LANES:(m + 1) * LANES] = _pair_cols(dec, lo, 2 * m)


def _mconv_sample(zx, dtr, buf, conv_w, conv_b, dtb, alog):
    bs, n = zx.shape
    tt = 32
    nb = buf.shape[1]
    row = lambda w: pl.BlockSpec((1, w), lambda i: (0, 0))
    tok = lambda w: pl.BlockSpec((tt, w), lambda i: (i, 0))
    return pl.pallas_call(
        _mconv_sample_kernel,
        grid=(bs // tt,),
        in_specs=[tok(n), tok(LANES), tok(nb),
                  pl.BlockSpec((M_CONV, M_CONV_DIM), lambda i: (0, 0)),
                  row(M_CONV_DIM), row(LANES), row(LANES)],
        out_specs=[tok(M_CONV_DIM), tok(nb), tok(M_INNER), tok(M_INNER)],
        out_shape=[jax.ShapeDtypeStruct((bs, M_CONV_DIM), F32),
                   jax.ShapeDtypeStruct((bs, nb), F32),
                   jax.ShapeDtypeStruct((bs, M_INNER), F32),
                   jax.ShapeDtypeStruct((bs, M_INNER), F32)],
        compiler_params=_cparams("parallel"),
        name="mconv_sample",
    )(zx, dtr, buf, conv_w, conv_b, dtb, alog)


def _ssd_sample_kernel(x_ref, b_ref, c_ref, z_ref, dte_ref, dece_ref, s_ref, dsk_ref, nw_ref,
                       y_ref, so_ref, *, tt):
    x = x_ref[...]
    bm = b_ref[...]
    cm = c_ref[...]
    lane = lax.broadcasted_iota(jnp.int32, (LANES, LANES), 1)
    ys = []
    for p in range(x.shape[1] // LANES):
        sl = slice(p * LANES, (p + 1) * LANES)
        xdt_t = _pad_t(x[:, sl] * dte_ref[:, sl])
        dec_t = _pad_t(dece_ref[:, sl])
        y_t = jnp.zeros((LANES, LANES), F32)
        for i in range(tt):
            xc = xdt_t[:, i:i + 1]
            dc = dec_t[:, i:i + 1]
            cols = []
            for j in range(2):
                rows = slice(j * M_HEADDIM, (j + 1) * M_HEADDIM)
                s_new = dc[rows] * s_ref[i, 2 * p + j] + xc[rows] * bm[i:i + 1, :]
                so_ref[i, 2 * p + j] = s_new
                cols.append(jnp.sum(s_new * cm[i:i + 1, :], axis=1, keepdims=True))
            y_t = jnp.where(lane == i, jnp.concatenate(cols, axis=0), y_t)
        ys.append(y_t.T[0:tt])
    y = jnp.concatenate(ys, axis=1) + dsk_ref[...] * x
    y = y * _silu(z_ref[...])
    y_ref[...] = _rms(y) * nw_ref[...]


def _ssd_sample(xbc, zx, dte, dece, state, dsk, nw):
    bs = xbc.shape[0]
    tt = 8
    gw = M_INNER // M_GROUPS
    hpg = M_HEADS // M_GROUPS
    return pl.pallas_call(
        functools.partial(_ssd_sample_kernel, tt=tt),
        grid=(bs // tt, M_GROUPS),
        in_specs=[pl.BlockSpec((tt, gw), lambda i, g: (i, g)),
                  pl.BlockSpec((tt, M_STATE), lambda i, g: (i, M_INNER // M_STATE + g)),
                  pl.BlockSpec((tt, M_STATE), lambda i, g: (i, (M_INNER + M_GN) // M_STATE + g)),
                  pl.BlockSpec((tt, gw), lambda i, g: (i, g)),
                  pl.BlockSpec((tt, gw), lambda i, g: (i, g)),
                  pl.BlockSpec((tt, gw), lambda i, g: (i, g)),
                  pl.BlockSpec((tt, hpg, M_HEADDIM, M_STATE), lambda i, g: (i, g, 0, 0)),
                  pl.BlockSpec((1, gw), lambda i, g: (0, g)),
                  pl.BlockSpec((1, gw), lambda i, g: (0, g))],
        out_specs=[pl.BlockSpec((tt, gw), lambda i, g: (i, g)),
                   pl.BlockSpec((tt, hpg, M_HEADDIM, M_STATE), lambda i, g: (i, g, 0, 0))],
        out_shape=[jax.ShapeDtypeStruct((bs, M_INNER), F32),
                   jax.ShapeDtypeStruct(state.shape, F32)],
        compiler_params=_cparams("parallel", "parallel"),
        name="ssd_sample",
    )(xbc, xbc, xbc, zx, dte, dece, state, dsk, nw)


def _pad_lanes(v):
    return jnp.pad(v.reshape(1, -1), ((0, 0), (0, LANES - v.shape[-1])))


def kernel(x_prompt, x_sample, c_prompt, c_sample, state_hgrn, state_shortconv, state_ssm, state_mconv, ada_w, ada_b, norm_mix, norm_mlp, norm_final, w_in_even, hgrn_lb, hgrn_gnorm, sc_w, w_out_even, w_in_odd, mconv_w, mconv_b, dt_bias, a_log, d_skip, m_norm, w_out_odd, mlp_w1, mlp_w2):
    bp = x_prompt.shape[0]
    bs = x_sample.shape[0]
    mod = _adaln_mod(jnp.concatenate([c_prompt, c_sample], axis=0), ada_w, ada_b)
    mod_p = [mod[l, :bp].reshape(bp, 1, -1) for l in range(2)]
    mod_s = [mod[l, bp:].reshape(1, bs, -1) for l in range(2)]

    w_even = w_in_even[0].astype(BF16)
    wo_even = w_out_even[0].astype(BF16)
    nzx = M_INNER + M_CONV_DIM
    w_zx = w_in_odd[0][:, :nzx].astype(BF16)
    w_dt = jnp.pad(w_in_odd[0][:, nzx:], ((0, 0), (0, LANES - M_HEADS)))
    wo_odd = w_out_odd[0].astype(BF16)
    w1 = mlp_w1.astype(BF16)
    w2 = mlp_w2.astype(BF16)
    dtb = _pad_lanes(dt_bias[0])
    alog = _pad_lanes(a_log[0])
    dsk = jnp.repeat(d_skip[0], M_HEADDIM).reshape(1, M_INNER)
    nw = m_norm[0].reshape(1, M_INNER)
    cb = mconv_b[0].reshape(1, M_CONV_DIM)

    (proj,) = _norm_proj(x_prompt, mod_p[0], norm_mix[0], w_even)
    o, hg_p, sc_p = _even_prompt(proj, hgrn_lb, hgrn_gnorm[0], sc_w[0], 1)
    x1 = _post(x_prompt, o, mod_p[0], norm_mlp[0], wo_even, w1[0], w2[0])
    zx, dtr = _norm_proj(x1, mod_p[1], norm_mix[1], w_zx, w_dt)
    y, ssm_p, mc_p = _mamba_prompt(zx, dtr, mconv_w[0], cb, dtb, alog, dsk, nw)
    y_prompt = _post(x1, y, mod_p[1], norm_mlp[1], wo_odd, w1[1], w2[1], norm_final)

    xs = x_sample.reshape(1, bs, D_MODEL)
    (proj_s,) = _norm_proj(xs, mod_s[0], norm_mix[0], w_even)
    o_s, hg_s, sc_s = _even_sample(proj_s[0], state_hgrn[0], state_shortconv[0].reshape(bs, -1),
                                   hgrn_lb, hgrn_gnorm[0], sc_w[0], 1)
    x1s = _post(xs, o_s[None], mod_s[0], norm_mlp[0], wo_even, w1[0], w2[0])
    zx_s, dtr_s = _norm_proj(x1s, mod_s[1], norm_mix[1], w_zx, w_dt)
    xbc_s, mc_s, dte, dece = _mconv_sample(zx_s[0], dtr_s[0], state_mconv[0].reshape(bs, -1),
                                           mconv_w[0], cb, dtb, alog)
    y_s, ssm_s = _ssd_sample(xbc_s, zx_s[0], dte, dece, state_ssm[0], dsk, nw)
    y_sample = _post(x1s, y_s[None], mod_s[1], norm_mlp[1], wo_odd, w1[1], w2[1], norm_final)

    return (y_prompt, y_sample.reshape(bs, 1, D_MODEL),
            hg_p[None], hg_s[None],
            sc_p[None], sc_s.reshape(1, bs, SC_WIDTH - 1, B_WIDTH),
            ssm_p[None], ssm_s[None],
            mc_p[None], mc_s.reshape(1, bs, M_CONV - 1, M_CONV_DIM))
```

```python
import functools

import jax
import jax.numpy as jnp
from jax import lax
from jax.experimental import pallas as pl
from jax.experimental.pallas import tpu as pltpu

F32 = jnp.float32
BF16 = jnp.bfloat16
EPS = 1e-6

D_MODEL = 1024
A_HEADS = 4
A_DK = 128
A_WIDTH = 512
B_WIDTH = 512
M_INNER = 2048
M_HEADDIM = 64
M_HEADS = 32
M_STATE = 128
M_GROUPS = 4
M_GN = M_GROUPS * M_STATE
M_CONV_DIM = M_INNER + 2 * M_GN
M_CONV = 4
SC_WIDTH = 3
D_FF = 4096

LANES = 128
GLA_CHUNK = 32
SSD_CHUNK = 128
V7X_VMEM_BYTES = 64 * 1024 * 1024
VMEM_LIMIT = V7X_VMEM_BYTES - 8 * 1024 * 1024


def _cparams(*sem):
    return pltpu.CompilerParams(dimension_semantics=sem, vmem_limit_bytes=VMEM_LIMIT)


def _rms(x):
    return x * lax.rsqrt(jnp.mean(x * x, axis=-1, keepdims=True) + EPS)


def _silu(x):
    return x * jax.nn.sigmoid(x)


def _softplus(x):
    return jnp.maximum(x, 0.0) + jnp.log1p(jnp.exp(-jnp.abs(x)))


def _bdot(a, b):
    return jnp.dot(a.astype(BF16), b.astype(BF16), preferred_element_type=F32)


def _bdot_nt(a, b):
    return lax.dot_general(a.astype(BF16), b.astype(BF16), (((1,), (1,)), ((), ())),
                           preferred_element_type=F32)


def _bdot_tn(a, b):
    return lax.dot_general(a.astype(BF16), b.astype(BF16), (((0,), (0,)), ((), ())),
                           preferred_element_type=F32)


def _dot3(m01, x):
    hi = x.astype(BF16)
    r = x - hi.astype(F32)
    mid = r.astype(BF16)
    lo = (r - mid.astype(F32)).astype(BF16)
    d = lambda v: jnp.dot(m01, v, preferred_element_type=F32)
    return d(hi) + d(mid) + d(lo)


def _tri_mask(n, seg):
    r = lax.broadcasted_iota(jnp.int32, (n, n), 0)
    c = lax.broadcasted_iota(jnp.int32, (n, n), 1)
    shift = seg.bit_length() - 1
    same = (r >> shift) == (c >> shift)
    return jnp.where(same, jnp.where(c <= r, 1.0, 0.0), 0.0).astype(BF16)


def _pad_t(a):
    r = a.shape[0]
    return jnp.concatenate([a, jnp.zeros((LANES - r, LANES), F32)], axis=0).T


def _hgrn_lower_bound(lb_all, slot):
    m = jnp.max(lb_all, axis=0, keepdims=True)
    e = jnp.exp(lb_all - m)
    p = e / jnp.sum(e, axis=0, keepdims=True)
    return jnp.sum(p[1:slot + 1], axis=0, keepdims=True)


def _mod_kernel(c_ref, w_ref, b_ref, o_ref):
    c = c_ref[...]
    a = _silu(c).astype(BF16)
    o_ref[0] = jnp.dot(a, w_ref[0].astype(BF16), preferred_element_type=F32) + b_ref[0]


def _adaln_mod(c_all, ada_w, ada_b):
    n_layers, d, n = ada_w.shape
    r = c_all.shape[0]
    tn = 1536
    return pl.pallas_call(
        _mod_kernel,
        grid=(n_layers, n // tn),
        in_specs=[pl.BlockSpec((r, d), lambda l, j: (0, 0)),
                  pl.BlockSpec((1, d, tn), lambda l, j: (l, 0, j)),
                  pl.BlockSpec((1, 1, tn), lambda l, j: (l, 0, j))],
        out_specs=pl.BlockSpec((1, r, tn), lambda l, j: (l, 0, j)),
        out_shape=jax.ShapeDtypeStruct((n_layers, r, n), F32),
        compiler_params=_cparams("parallel", "parallel"),
        name="adaln_mod",
    )(c_all, ada_w, ada_b.reshape(n_layers, 1, n))


def _mod_spec(mod, tm, col):
    if mod.shape[1] == 1:
        return pl.BlockSpec((1, 1, D_MODEL), lambda b, t: (b, 0, col))
    return pl.BlockSpec((1, tm, D_MODEL), lambda b, t: (b, t, col))


def _norm_proj_kernel(x_ref, sh_ref, sc_ref, g_ref, w_ref, *rest, has_dt):
    x = x_ref[0]
    h = (_rms(x) * g_ref[...]) * (1.0 + sc_ref[0]) + sh_ref[0]
    hb = h.astype(BF16)
    if has_dt:
        wdt_ref, o_ref, odt_ref = rest
        h_lo = (h - hb.astype(F32)).astype(BF16)
        a = jnp.dot(hb, wdt_ref[...], preferred_element_type=F32)
        odt_ref[0] = (a[:, :LANES] + a[:, LANES:]
                      + jnp.dot(h_lo, wdt_ref[:, :LANES], preferred_element_type=F32))
    else:
        (o_ref,) = rest
    o_ref[0] = jnp.dot(hb, w_ref[...], preferred_element_type=F32)


def _norm_proj(x, mod, g, w, w_dt=None):
    bn, t, d = x.shape
    tm = min(t, 512)
    n = w.shape[1]
    has_dt = w_dt is not None
    in_specs = [pl.BlockSpec((1, tm, d), lambda b, i: (b, i, 0)),
                _mod_spec(mod, tm, 0), _mod_spec(mod, tm, 1),
                pl.BlockSpec((1, d), lambda b, i: (0, 0)),
                pl.BlockSpec((d, n), lambda b, i: (0, 0), pipeline_mode=pl.Buffered(1))]
    out_specs = [pl.BlockSpec((1, tm, n), lambda b, i: (b, i, 0))]
    out_shape = [jax.ShapeDtypeStruct((bn, t, n), F32)]
    args = [x, mod, mod, g.reshape(1, d), w]
    if has_dt:
        in_specs.append(pl.BlockSpec((d, 2 * LANES), lambda b, i: (0, 0)))
        out_specs.append(pl.BlockSpec((1, tm, LANES), lambda b, i: (b, i, 0)))
        out_shape.append(jax.ShapeDtypeStruct((bn, t, LANES), F32))
        args.append(w_dt)
    outs = pl.pallas_call(
        functools.partial(_norm_proj_kernel, has_dt=has_dt),
        grid=(bn, t // tm),
        in_specs=in_specs, out_specs=out_specs, out_shape=out_shape,
        compiler_params=_cparams("parallel", "parallel"),
        name="norm_proj_dt" if has_dt else "norm_proj",
    )(*args)
    return outs


def _post_kernel(x_ref, o_ref, g1_ref, sh2_ref, s2_ref, g2_ref, nm_ref, wo_ref, w1_ref, w2_ref,
                 *rest, final, fc):
    x = x_ref[0]
    x1 = x + g1_ref[0] * jnp.dot(o_ref[0].astype(BF16), wo_ref[...], preferred_element_type=F32)
    h = ((_rms(x1) * nm_ref[...]) * (1.0 + s2_ref[0]) + sh2_ref[0]).astype(BF16)
    acc = jnp.zeros_like(x1)
    for c in range(D_FF // fc):
        a = jnp.dot(h, w1_ref[:, c * fc:(c + 1) * fc], preferred_element_type=F32)
        a = jnp.square(jnp.maximum(a, 0.0)).astype(BF16)
        acc = acc + jnp.dot(a, w2_ref[c * fc:(c + 1) * fc, :], preferred_element_type=F32)
    x2 = x1 + g2_ref[0] * acc
    if final:
        nf_ref, out_ref = rest
        out_ref[0] = _rms(x2) * nf_ref[...]
    else:
        (out_ref,) = rest
        out_ref[0] = x2


def _post(x, o, mod, nm, wo, w1, w2, nf=None):
    bn, t, d = x.shape
    tm = min(t, 512)
    kin = o.shape[2]
    final = nf is not None
    const = lambda shape: pl.BlockSpec(shape, lambda b, i: (0,) * len(shape), pipeline_mode=pl.Buffered(1))
    in_specs = [pl.BlockSpec((1, tm, d), lambda b, i: (b, i, 0)),
                pl.BlockSpec((1, tm, kin), lambda b, i: (b, i, 0)),
                _mod_spec(mod, tm, 2), _mod_spec(mod, tm, 3), _mod_spec(mod, tm, 4), _mod_spec(mod, tm, 5),
                pl.BlockSpec((1, d), lambda b, i: (0, 0)),
                const((kin, d)), const((d, D_FF)), const((D_FF, d))]
    args = [x, o, mod, mod, mod, mod, nm.reshape(1, d), wo, w1, w2]
    if final:
        in_specs.append(pl.BlockSpec((1, d), lambda b, i: (0, 0)))
        args.append(nf.reshape(1, d))
    return pl.pallas_call(
        functools.partial(_post_kernel, final=final, fc=1024),
        grid=(bn, t // tm),
        in_specs=in_specs,
        out_specs=pl.BlockSpec((1, tm, d), lambda b, i: (b, i, 0)),
        out_shape=jax.ShapeDtypeStruct((bn, t, d), F32),
        compiler_params=_cparams("parallel", "parallel"),
        name="post_final" if final else "post",
    )(*args)


def _even_prompt_kernel(proj_ref, lb_ref, gn_ref, scw_ref, o_ref, s_ref, sc_ref, st_scr, ubuf,
                        *, tb, slot):
    t = pl.program_id(1)

    @pl.when(t == 0)
    def _():
        st_scr[...] = jnp.zeros_like(st_scr)
        ubuf[0:8, :] = jnp.zeros((8, B_WIDTH), F32)

    lb = _hgrn_lower_bound(lb_ref[...], slot)
    tri = _tri_mask(tb, GLA_CHUNK)
    r = lax.broadcasted_iota(jnp.int32, (GLA_CHUNK, GLA_CHUNK), 0)
    c = lax.broadcasted_iota(jnp.int32, (GLA_CHUNK, GLA_CHUNK), 1)
    causal = c <= r
    gn = gn_ref[...]
    n_chunks = tb // GLA_CHUNK
    for h in range(A_HEADS):
        cols = lambda part: slice(part * A_WIDTH + h * A_DK, part * A_WIDTH + (h + 1) * A_DK)
        q = proj_ref[0, :, cols(0)]
        fz = proj_ref[0, :, cols(1)]
        v = proj_ref[0, :, cols(2)]
        go = proj_ref[0, :, cols(3)]
        lbh = lb[:, h * A_DK:(h + 1) * A_DK]
        f = lbh + (1.0 - lbh) * jax.nn.sigmoid(fz)
        k = 1.0 - f
        b = _dot3(tri, jnp.log(f))
        st = st_scr[h]
        for n in range(n_chunks):
            rows = slice(n * GLA_CHUNK, (n + 1) * GLA_CHUNK)
            bn = b[rows]
            bl = b[(n + 1) * GLA_CHUNK - 1:(n + 1) * GLA_CHUNK]
            qd = q[rows] * jnp.exp(bn)
            kd = k[rows] * jnp.exp(-bn)
            kdec = k[rows] * jnp.exp(bl - bn)
            att = jnp.where(causal, _bdot_nt(qd, kd), 0.0)
            o = _bdot(att, v[rows]) + _bdot_nt(qd, st)
            st = st * jnp.exp(bl) + _bdot_tn(v[rows], kdec)
            o = (_rms(o) * gn) * _silu(go[rows])
            o_ref[0, rows, h * A_DK:(h + 1) * A_DK] = o.astype(BF16)
        st_scr[h] = st

    off = 4 * A_WIDTH
    bg = proj_ref[0, :, off:off + B_WIDTH]
    cg = proj_ref[0, :, off + B_WIDTH:off + 2 * B_WIDTH]
    hv = proj_ref[0, :, off + 2 * B_WIDTH:off + 3 * B_WIDTH]
    u = cg * hv
    ubuf[8:8 + tb, :] = u
    w = scw_ref[...]
    conv = ubuf[6:6 + tb, :] * w[0:1] + ubuf[7:7 + tb, :] * w[1:2] + u * w[2:3]
    o_ref[0, :, A_WIDTH:A_WIDTH + B_WIDTH] = (bg * conv).astype(BF16)
    ubuf[6:8, :] = u[tb - 2:tb]

    @pl.when(t == pl.num_programs(1) - 1)
    def _():
        for h in range(A_HEADS):
            s_ref[0, h] = st_scr[h].T
        sc_ref[0] = u[tb - 2:tb]


def _even_prompt(proj, hgrn_lb, gnorm, sc_w, slot):
    bn, t, n = proj.shape
    tb = 256
    return pl.pallas_call(
        functools.partial(_even_prompt_kernel, tb=tb, slot=slot),
        grid=(bn, t // tb),
        in_specs=[pl.BlockSpec((1, tb, n), lambda b, i: (b, i, 0)),
                  pl.BlockSpec(hgrn_lb.shape, lambda b, i: (0, 0)),
                  pl.BlockSpec((1, A_DK), lambda b, i: (0, 0)),
                  pl.BlockSpec((SC_WIDTH, B_WIDTH), lambda b, i: (0, 0))],
        out_specs=[pl.BlockSpec((1, tb, D_MODEL), lambda b, i: (b, i, 0)),
                   pl.BlockSpec((1, A_HEADS, A_DK, A_DK), lambda b, i: (b, 0, 0, 0)),
                   pl.BlockSpec((1, SC_WIDTH - 1, B_WIDTH), lambda b, i: (b, 0, 0))],
        out_shape=[jax.ShapeDtypeStruct((bn, t, D_MODEL), BF16),
                   jax.ShapeDtypeStruct((bn, A_HEADS, A_DK, A_DK), F32),
                   jax.ShapeDtypeStruct((bn, SC_WIDTH - 1, B_WIDTH), F32)],
        scratch_shapes=[pltpu.VMEM((A_HEADS, A_DK, A_DK), F32),
                        pltpu.VMEM((tb + 8, B_WIDTH), F32)],
        compiler_params=_cparams("parallel", "arbitrary"),
        name="even_prompt",
    )(proj, hgrn_lb, gnorm.reshape(1, A_DK), sc_w)


def _even_sample_kernel(proj_ref, s_ref, scb_ref, lb_ref, gn_ref, scw_ref, o_ref, so_ref, sco_ref,
                        *, tt, slot):
    lb = _hgrn_lower_bound(lb_ref[...], slot)
    gn = gn_ref[...]
    for h in range(A_HEADS):
        cols = lambda part: slice(part * A_WIDTH + h * A_DK, part * A_WIDTH + (h + 1) * A_DK)
        q = proj_ref[:, cols(0)]
        fz = proj_ref[:, cols(1)]
        v = proj_ref[:, cols(2)]
        go = proj_ref[:, cols(3)]
        lbh = lb[:, h * A_DK:(h + 1) * A_DK]
        f = lbh + (1.0 - lbh) * jax.nn.sigmoid(fz)
        f_t = _pad_t(f)
        q_t = _pad_t(q)
        outs = []
        for i in range(tt):
            fc = f_t[:, i:i + 1]
            s_new = fc * s_ref[i, h] + (1.0 - fc) * v[i:i + 1, :]
            so_ref[i, h] = s_new
            outs.append(jnp.sum(q_t[:, i:i + 1] * s_new, axis=0, keepdims=True))
        o = jnp.concatenate(outs, axis=0)
        o_ref[:, h * A_DK:(h + 1) * A_DK] = (_rms(o) * gn) * _silu(go)

    off = 4 * A_WIDTH
    bg = proj_ref[:, off:off + B_WIDTH]
    u = proj_ref[:, off + B_WIDTH:off + 2 * B_WIDTH] * proj_ref[:, off + 2 * B_WIDTH:off + 3 * B_WIDTH]
    w = scw_ref[...]
    conv = scb_ref[:, 0:B_WIDTH] * w[0:1] + scb_ref[:, B_WIDTH:2 * B_WIDTH] * w[1:2] + u * w[2:3]
    o_ref[:, A_WIDTH:A_WIDTH + B_WIDTH] = bg * conv
    sco_ref[:, 0:B_WIDTH] = scb_ref[:, B_WIDTH:2 * B_WIDTH]
    sco_ref[:, B_WIDTH:2 * B_WIDTH] = u


def _even_sample(proj, state, sc_buf, hgrn_lb, gnorm, sc_w, slot):
    bs, n = proj.shape
    tt = 8
    nb = sc_buf.shape[1]
    return pl.pallas_call(
        functools.partial(_even_sample_kernel, tt=tt, slot=slot),
        grid=(bs // tt,),
        in_specs=[pl.BlockSpec((tt, n), lambda i: (i, 0)),
                  pl.BlockSpec((tt, A_HEADS, A_DK, A_DK), lambda i: (i, 0, 0, 0)),
                  pl.BlockSpec((tt, nb), lambda i: (i, 0)),
                  pl.BlockSpec(hgrn_lb.shape, lambda i: (0, 0)),
                  pl.BlockSpec((1, A_DK), lambda i: (0, 0)),
                  pl.BlockSpec((SC_WIDTH, B_WIDTH), lambda i: (0, 0))],
        out_specs=[pl.BlockSpec((tt, D_MODEL), lambda i: (i, 0)),
                   pl.BlockSpec((tt, A_HEADS, A_DK, A_DK), lambda i: (i, 0, 0, 0)),
                   pl.BlockSpec((tt, nb), lambda i: (i, 0))],
        out_shape=[jax.ShapeDtypeStruct((bs, D_MODEL), F32),
                   jax.ShapeDtypeStruct(state.shape, F32),
                   jax.ShapeDtypeStruct(sc_buf.shape, F32)],
        compiler_params=_cparams("parallel"),
        name="even_sample",
    )(proj, state, sc_buf, hgrn_lb, gnorm.reshape(1, A_DK), sc_w)


def _pair_cols(m, lo, h0):
    return jnp.where(lo, m[:, h0:h0 + 1], m[:, h0 + 1:h0 + 2])


def _mamba_prompt_kernel(zx_ref, dtr_ref, cw_ref, cb_ref, dtb_ref, alog_ref, dsk_ref, nw_ref,
                         y_ref, s_ref, mc_ref, st_scr, cbuf):
    t = pl.program_id(1)
    q = SSD_CHUNK

    @pl.when(t == 0)
    def _():
        st_scr[...] = jnp.zeros_like(st_scr)
        cbuf[0:8, :] = jnp.zeros((8, M_CONV_DIM), F32)

    xr = zx_ref[0, :, M_INNER:M_INNER + M_CONV_DIM]
    cbuf[8:8 + q, :] = xr
    cw = cw_ref[...]
    conv = (cbuf[5:5 + q, :] * cw[0:1] + cbuf[6:6 + q, :] * cw[1:2]
            + cbuf[7:7 + q, :] * cw[2:3] + xr * cw[3:4])
    cbuf[5:8, :] = xr[q - 3:q]
    xbc = _silu(conv + cb_ref[...])

    dt = _softplus(dtr_ref[0] + dtb_ref[...])
    d_a = dt * (-jnp.exp(alog_ref[...]))
    cs = _dot3(_tri_mask(q, q), d_a)
    cs_t = cs.T
    dt_t = dt.T
    ecs = jnp.exp(cs)
    cs_last = cs[q - 1:q, :]
    wx = jnp.exp(cs_last - cs) * dt
    dec = jnp.exp(cs_last)

    r = lax.broadcasted_iota(jnp.int32, (q, q), 0)
    c = lax.broadcasted_iota(jnp.int32, (q, q), 1)
    causal = c <= r
    lo = c < M_HEADDIM
    lo1 = lo[0:1]
    gw = M_INNER // M_GROUPS
    for g in range(M_GROUPS):
        bm = xbc[:, M_INNER + g * M_STATE:M_INNER + (g + 1) * M_STATE]
        cm = xbc[:, M_INNER + M_GN + g * M_STATE:M_INNER + M_GN + (g + 1) * M_STATE]
        cb = _bdot_nt(cm, bm)
        st_g = st_scr[:, g * gw:(g + 1) * gw]
        y_off = _bdot(cm, st_g)
        ys, xws, decs = [], [], []
        for p in range(gw // LANES):
            m = g * (gw // LANES) + p
            h0 = 2 * m
            xp = xbc[:, m * LANES:(m + 1) * LANES]
            yd = []
            for h in (h0, h0 + 1):
                seg = jnp.minimum(cs[:, h:h + 1] - cs_t[h:h + 1, :], 0.0)
                w = cb * jnp.where(causal, jnp.exp(seg), 0.0) * dt_t[h:h + 1, :]
                yd.append(_bdot(w, xp))
            y = jnp.where(lo, yd[0], yd[1])
            y = y + y_off[:, p * LANES:(p + 1) * LANES] * _pair_cols(ecs, lo, h0)
            ys.append(y + dsk_ref[:, m * LANES:(m + 1) * LANES] * xp)
            xws.append(xp * _pair_cols(wx, lo, h0))
            decs.append(_pair_cols(dec, lo1, h0))
        xw = jnp.concatenate(xws, axis=1)
        st_scr[:, g * gw:(g + 1) * gw] = st_g * jnp.concatenate(decs, axis=1) + _bdot(bm.T, xw)
        y = jnp.concatenate(ys, axis=1) * _silu(zx_ref[0, :, g * gw:(g + 1) * gw])
        y_ref[0, :, g * gw:(g + 1) * gw] = (_rms(y) * nw_ref[:, g * gw:(g + 1) * gw]).astype(BF16)

    @pl.when(t == pl.num_programs(1) - 1)
    def _():
        for m in range(M_INNER // LANES):
            blk = st_scr[:, m * LANES:(m + 1) * LANES].T
            s_ref[0, 2 * m] = blk[0:M_HEADDIM]
            s_ref[0, 2 * m + 1] = blk[M_HEADDIM:2 * M_HEADDIM]
        mc_ref[0] = xr[q - 3:q]


def _mamba_prompt(zx, dtr, conv_w, conv_b, dtb, alog, dsk, nw):
    bn, t, n = zx.shape
    q = SSD_CHUNK
    row = lambda w: pl.BlockSpec((1, w), lambda b, i: (0, 0))
    return pl.pallas_call(
        _mamba_prompt_kernel,
        grid=(bn, t // q),
        in_specs=[pl.BlockSpec((1, q, n), lambda b, i: (b, i, 0)),
                  pl.BlockSpec((1, q, LANES), lambda b, i: (b, i, 0)),
                  pl.BlockSpec((M_CONV, M_CONV_DIM), lambda b, i: (0, 0)),
                  row(M_CONV_DIM), row(LANES), row(LANES), row(M_INNER), row(M_INNER)],
        out_specs=[pl.BlockSpec((1, q, M_INNER), lambda b, i: (b, i, 0)),
                   pl.BlockSpec((1, M_HEADS, M_HEADDIM, M_STATE), lambda b, i: (b, 0, 0, 0)),
                   pl.BlockSpec((1, M_CONV - 1, M_CONV_DIM), lambda b, i: (b, 0, 0))],
        out_shape=[jax.ShapeDtypeStruct((bn, t, M_INNER), BF16),
                   jax.ShapeDtypeStruct((bn, M_HEADS, M_HEADDIM, M_STATE), F32),
                   jax.ShapeDtypeStruct((bn, M_CONV - 1, M_CONV_DIM), F32)],
        scratch_shapes=[pltpu.VMEM((M_STATE, M_INNER), F32),
                        pltpu.VMEM((q + 8, M_CONV_DIM), F32)],
        compiler_params=_cparams("parallel", "arbitrary"),
        name="mamba_prompt",
    )(zx, dtr, conv_w, conv_b, dtb, alog, dsk, nw)


def _mconv_sample_kernel(zx_ref, dtr_ref, buf_ref, cw_ref, cb_ref, dtb_ref, alog_ref,
                         xbc_ref, nbuf_ref, dte_ref, decb_ref):
    n = M_CONV_DIM
    u = zx_ref[:, M_INNER:M_INNER + n]
    cw = cw_ref[...]
    conv = (buf_ref[:, 0:n] * cw[0:1] + buf_ref[:, n:2 * n] * cw[1:2]
            + buf_ref[:, 2 * n:3 * n] * cw[2:3] + u * cw[3:4])
    xbc_ref[...] = _silu(conv + cb_ref[...])
    nbuf_ref[:, 0:2 * n] = buf_ref[:, n:3 * n]
    nbuf_ref[:, 2 * n:3 * n] = u
    dt = _softplus(dtr_ref[...] + dtb_ref[...])
    dec = jnp.exp(dt * (-jnp.exp(alog_ref[...])))
    tt = dt.shape[0]
    lo = lax.broadcasted_iota(jnp.int32, (tt, LANES), 1) < M_HEADDIM
    for m in range(M_INNER // LANES):
        dte_ref[:, m * LANES:(m + 1) * LANES] = _pair_cols(dt, lo, 2 * m)
    for h in range(M_HEADS):
        decb_ref[h] = jnp.broadcast_to(dec[:, h:h + 1], (tt, LANES))


def _mconv_sample(zx, dtr, buf, conv_w, conv_b, dtb, alog):
    bs, n = zx.shape
    tt = 32
    nb = buf.shape[1]
    row = lambda w: pl.BlockSpec((1, w), lambda i: (0, 0))
    tok = lambda w: pl.BlockSpec((tt, w), lambda i: (i, 0))
    return pl.pallas_call(
        _mconv_sample_kernel,
        grid=(bs // tt,),
        in_specs=[tok(n), tok(LANES), tok(nb),
                  pl.BlockSpec((M_CONV, M_CONV_DIM), lambda i: (0, 0)),
                  row(M_CONV_DIM), row(LANES), row(LANES)],
        out_specs=[tok(M_CONV_DIM), tok(nb), tok(M_INNER),
                   pl.BlockSpec((M_HEADS, tt, LANES), lambda i: (0, i, 0))],
        out_shape=[jax.ShapeDtypeStruct((bs, M_CONV_DIM), F32),
                   jax.ShapeDtypeStruct((bs, nb), F32),
                   jax.ShapeDtypeStruct((bs, M_INNER), F32),
                   jax.ShapeDtypeStruct((M_HEADS, bs, LANES), F32)],
        compiler_params=_cparams("parallel"),
        name="mconv_sample",
    )(zx, dtr, buf, conv_w, conv_b, dtb, alog)


def _split3(x):
    hi = x.astype(BF16).astype(F32)
    r = x - hi
    mid = r.astype(BF16).astype(F32)
    return hi, mid, (r - mid).astype(BF16).astype(F32)


def _ssd_sample_kernel(x_ref, b_ref, c_ref, z_ref, dte_ref, decb_ref, s_ref, dsk_ref, nw_ref,
                       y_ref, so_ref, *, tt):
    x = x_ref[...]
    gw = x.shape[1]
    hpg = gw // M_HEADDIM
    xh, xm, xl = _split3(x * dte_ref[...])
    bh, bmid, bl = _split3(b_ref[...])
    n_terms = 6
    lhs_t = jnp.concatenate([xh, xh, xm, xh, xm, xl, jnp.zeros((LANES - n_terms * tt, gw), F32)], axis=0)
    lhs = jnp.concatenate([lhs_t[:, p * LANES:(p + 1) * LANES].T for p in range(gw // LANES)],
                          axis=0).astype(BF16)
    rhs_all = jnp.concatenate([bh, bmid, bh, bl, bmid, bh,
                               jnp.zeros((LANES - n_terms * tt, M_STATE), F32)], axis=0)
    row_tok = lax.broadcasted_iota(jnp.int32, (LANES, M_STATE), 0) & (tt - 1)
    cmb = c_ref[...].astype(BF16)
    out_row = lax.broadcasted_iota(jnp.int32, (tt, gw), 0)
    y = jnp.zeros((tt, gw), F32)
    for i in range(tt):
        rhs = jnp.where(row_tok == i, rhs_all, 0.0).astype(BF16)
        xb = jnp.dot(lhs, rhs, preferred_element_type=F32)
        pieces = []
        for hh in range(hpg):
            rows = slice(hh * M_HEADDIM, (hh + 1) * M_HEADDIM)
            s_new = decb_ref[hh, i:i + 1, :] * s_ref[i, hh] + xb[rows]
            so_ref[i, hh] = s_new
            pieces.append(s_new.astype(BF16))
        yi = lax.dot_general(cmb, jnp.concatenate(pieces, axis=0), (((1,), (1,)), ((), ())),
                             preferred_element_type=F32)
        y = jnp.where(out_row == i, yi, y)
    y = (y + dsk_ref[...] * x) * _silu(z_ref[...])
    y_ref[...] = _rms(y) * nw_ref[...]


def _ssd_sample(xbc, zx, dte, decb, state, dsk, nw):
    bs = xbc.shape[0]
    tt = 8
    gw = M_INNER // M_GROUPS
    hpg = M_HEADS // M_GROUPS
    return pl.pallas_call(
        functools.partial(_ssd_sample_kernel, tt=tt),
        grid=(bs // tt, M_GROUPS),
        in_specs=[pl.BlockSpec((tt, gw), lambda i, g: (i, g)),
                  pl.BlockSpec((tt, M_STATE), lambda i, g: (i, M_INNER // M_STATE + g)),
                  pl.BlockSpec((tt, M_STATE), lambda i, g: (i, (M_INNER + M_GN) // M_STATE + g)),
                  pl.BlockSpec((tt, gw), lambda i, g: (i, g)),
                  pl.BlockSpec((tt, gw), lambda i, g: (i, g)),
                  pl.BlockSpec((hpg, tt, LANES), lambda i, g: (g, i, 0)),
                  pl.BlockSpec((tt, hpg, M_HEADDIM, M_STATE), lambda i, g: (i, g, 0, 0)),
                  pl.BlockSpec((1, gw), lambda i, g: (0, g)),
                  pl.BlockSpec((1, gw), lambda i, g: (0, g))],
        out_specs=[pl.BlockSpec((tt, gw), lambda i, g: (i, g)),
                   pl.BlockSpec((tt, hpg, M_HEADDIM, M_STATE), lambda i, g: (i, g, 0, 0))],
        out_shape=[jax.ShapeDtypeStruct((bs, M_INNER), F32),
                   jax.ShapeDtypeStruct(state.shape, F32)],
        compiler_params=_cparams("parallel", "parallel"),
        name="ssd_sample",
    )(xbc, xbc, xbc, zx, dte, decb, state, dsk, nw)


def _pad_lanes(v):
    return jnp.pad(v.reshape(1, -1), ((0, 0), (0, LANES - v.shape[-1])))


def kernel(x_prompt, x_sample, c_prompt, c_sample, state_hgrn, state_shortconv, state_ssm, state_mconv, ada_w, ada_b, norm_mix, norm_mlp, norm_final, w_in_even, hgrn_lb, hgrn_gnorm, sc_w, w_out_even, w_in_odd, mconv_w, mconv_b, dt_bias, a_log, d_skip, m_norm, w_out_odd, mlp_w1, mlp_w2):
    bp = x_prompt.shape[0]
    bs = x_sample.shape[0]
    mod = _adaln_mod(jnp.concatenate([c_prompt, c_sample], axis=0), ada_w, ada_b)
    mod_p = [mod[l, :bp].reshape(bp, 1, -1) for l in range(2)]
    mod_s = [mod[l, bp:].reshape(1, bs, -1) for l in range(2)]

    w_even = w_in_even[0].astype(BF16)
    wo_even = w_out_even[0].astype(BF16)
    nzx = M_INNER + M_CONV_DIM
    w_zx = w_in_odd[0][:, :nzx].astype(BF16)
    w_dt = jnp.pad(w_in_odd[0][:, nzx:], ((0, 0), (0, LANES - M_HEADS)))
    w_dt_hi = w_dt.astype(BF16)
    w_dt = jnp.concatenate([w_dt_hi, (w_dt - w_dt_hi.astype(F32)).astype(BF16)], axis=1)
    wo_odd = w_out_odd[0].astype(BF16)
    w1 = mlp_w1.astype(BF16)
    w2 = mlp_w2.astype(BF16)
    dtb = _pad_lanes(dt_bias[0])
    alog = _pad_lanes(a_log[0])
    dsk = jnp.repeat(d_skip[0], M_HEADDIM).reshape(1, M_INNER)
    nw = m_norm[0].reshape(1, M_INNER)
    cb = mconv_b[0].reshape(1, M_CONV_DIM)

    (proj,) = _norm_proj(x_prompt, mod_p[0], norm_mix[0], w_even)
    o, hg_p, sc_p = _even_prompt(proj, hgrn_lb, hgrn_gnorm[0], sc_w[0], 1)
    x1 = _post(x_prompt, o, mod_p[0], norm_mlp[0], wo_even, w1[0], w2[0])
    zx, dtr = _norm_proj(x1, mod_p[1], norm_mix[1], w_zx, w_dt)
    y, ssm_p, mc_p = _mamba_prompt(zx, dtr, mconv_w[0], cb, dtb, alog, dsk, nw)
    y_prompt = _post(x1, y, mod_p[1], norm_mlp[1], wo_odd, w1[1], w2[1], norm_final)

    xs = x_sample.reshape(1, bs, D_MODEL)
    (proj_s,) = _norm_proj(xs, mod_s[0], norm_mix[0], w_even)
    o_s, hg_s, sc_s = _even_sample(proj_s[0], state_hgrn[0], state_shortconv[0].reshape(bs, -1),
                                   hgrn_lb, hgrn_gnorm[0], sc_w[0], 1)
    x1s = _post(xs, o_s[None], mod_s[0], norm_mlp[0], wo_even, w1[0], w2[0])
    zx_s, dtr_s = _norm_proj(x1s, mod_s[1], norm_mix[1], w_zx, w_dt)
    xbc_s, mc_s, dte, decb = _mconv_sample(zx_s[0], dtr_s[0], state_mconv[0].reshape(bs, -1),
                                           mconv_w[0], cb, dtb, alog)
    y_s, ssm_s = _ssd_sample(xbc_s, zx_s[0], dte, decb, state_ssm[0], dsk, nw)
    y_sample = _post(x1s, y_s[None], mod_s[1], norm_mlp[1], wo_odd, w1[1], w2[1], norm_final)

    return (y_prompt, y_sample.reshape(bs, 1, D_MODEL),
            hg_p[None], hg_s[None],
            sc_p[None], sc_s.reshape(1, bs, SC_WIDTH - 1, B_WIDTH),
            ssm_p[None], ssm_s[None],
            mc_p[None], mc_s.reshape(1, bs, M_CONV - 1, M_CONV_DIM))
```

```python
import functools

import jax
import jax.numpy as jnp
from jax import lax
from jax.experimental import pallas as pl
from jax.experimental.pallas import tpu as pltpu

F32 = jnp.float32
BF16 = jnp.bfloat16
EPS = 1e-6

D_MODEL = 1024
A_HEADS = 4
A_DK = 128
A_WIDTH = 512
B_WIDTH = 512
M_INNER = 2048
M_HEADDIM = 64
M_HEADS = 32
M_STATE = 128
M_GROUPS = 4
M_GN = M_GROUPS * M_STATE
M_CONV_DIM = M_INNER + 2 * M_GN
M_CONV = 4
SC_WIDTH = 3
D_FF = 4096

LANES = 128
SUBLANES = 8
GLA_CHUNK = 32
SSD_CHUNK = 128
V7X_VMEM_BYTES = 64 * 1024 * 1024
VMEM_LIMIT = V7X_VMEM_BYTES - 8 * 1024 * 1024
TOKEN_TILE = 512
GLA_TILE = 256
COL_BLOCK = 512


def _cparams(*sem):
    return pltpu.CompilerParams(dimension_semantics=sem, vmem_limit_bytes=VMEM_LIMIT)


def _rms(x):
    return x * lax.rsqrt(jnp.mean(x * x, axis=-1, keepdims=True) + EPS)


def _silu(x):
    return x * jax.nn.sigmoid(x)


def _softplus(x):
    return jnp.maximum(x, 0.0) + jnp.log1p(jnp.exp(-jnp.abs(x)))


def _bdot(a, b):
    return jnp.dot(a.astype(BF16), b.astype(BF16), preferred_element_type=F32)


def _bdot_nt(a, b):
    return lax.dot_general(a.astype(BF16), b.astype(BF16), (((1,), (1,)), ((), ())),
                           preferred_element_type=F32)


def _dot3(m01, x):
    hi = x.astype(BF16)
    r = x - hi.astype(F32)
    mid = r.astype(BF16)
    lo = (r - mid.astype(F32)).astype(BF16)
    d = lambda v: jnp.dot(m01, v, preferred_element_type=F32)
    return d(hi) + d(mid) + d(lo)


def _split3(x):
    hi = x.astype(BF16).astype(F32)
    r = x - hi
    mid = r.astype(BF16).astype(F32)
    return hi, mid, (r - mid).astype(BF16).astype(F32)


def _seg_causal(n, seg):
    r = lax.broadcasted_iota(jnp.int32, (n, n), 0)
    c = lax.broadcasted_iota(jnp.int32, (n, n), 1)
    shift = seg.bit_length() - 1
    return jnp.logical_and(c <= r, c >= ((r >> shift) << shift))


def _pad_t(a):
    r = a.shape[0]
    return jnp.concatenate([a, jnp.zeros((LANES - r, LANES), F32)], axis=0).T


def _hgrn_lower_bound(lb_all, slot):
    m = jnp.max(lb_all, axis=0, keepdims=True)
    e = jnp.exp(lb_all - m)
    p = e / jnp.sum(e, axis=0, keepdims=True)
    return jnp.sum(p[1:slot + 1], axis=0, keepdims=True)


def _shift_rows(u, tail, s):
    rolled = pltpu.roll(u, s, axis=0)
    row = lax.broadcasted_iota(jnp.int32, tail.shape, 0)
    head = jnp.where(row < s, pltpu.roll(tail, s, axis=0), rolled[0:SUBLANES])
    return jnp.concatenate([head, rolled[SUBLANES:]], axis=0)


def _modulated_norm(x_ref, sh_ref, sc_ref, g_ref):
    return (_rms(x_ref[0]) * g_ref[...]) * (1.0 + sc_ref[0]) + sh_ref[0]


def _dt_proj(h, hb, wdt_ref):
    h_lo = (h - hb.astype(F32)).astype(BF16)
    a = jnp.dot(hb, wdt_ref[...], preferred_element_type=F32)
    return a[:, :LANES] + a[:, LANES:] + jnp.dot(h_lo, wdt_ref[:, :LANES], preferred_element_type=F32)


def _mod_kernel(c_ref, w_ref, b_ref, o_ref):
    c = c_ref[...]
    a = _silu(c).astype(BF16)
    o_ref[0] = jnp.dot(a, w_ref[0].astype(BF16), preferred_element_type=F32) + b_ref[0]


def _adaln_mod(c_all, ada_w, ada_b):
    n_layers, d, n = ada_w.shape
    r = c_all.shape[0]
    tn = 1536
    return pl.pallas_call(
        _mod_kernel,
        grid=(n_layers, n // tn),
        in_specs=[pl.BlockSpec((r, d), lambda l, j: (0, 0)),
                  pl.BlockSpec((1, d, tn), lambda l, j: (l, 0, j)),
                  pl.BlockSpec((1, 1, tn), lambda l, j: (l, 0, j))],
        out_specs=pl.BlockSpec((1, r, tn), lambda l, j: (l, 0, j)),
        out_shape=jax.ShapeDtypeStruct((n_layers, r, n), F32),
        compiler_params=_cparams("parallel", "parallel"),
        name="adaln_mod",
    )(c_all, ada_w, ada_b.reshape(n_layers, 1, n))


def _mod_spec(mod, tm, col):
    if mod.shape[1] == 1:
        return pl.BlockSpec((1, 1, D_MODEL), lambda b, t: (b, 0, col))
    return pl.BlockSpec((1, tm, D_MODEL), lambda b, t: (b, t, col))


def _const_spec(shape):
    return pl.BlockSpec(shape, lambda b, i: (0,) * len(shape), pipeline_mode=pl.Buffered(1))


def _row_spec(w):
    return pl.BlockSpec((1, w), lambda b, i: (0, 0))


def _norm_proj_kernel(x_ref, sh_ref, sc_ref, g_ref, w_ref, *rest, has_dt):
    h = _modulated_norm(x_ref, sh_ref, sc_ref, g_ref)
    hb = h.astype(BF16)
    if has_dt:
        wdt_ref, o_ref, odt_ref = rest
        odt_ref[0] = _dt_proj(h, hb, wdt_ref)
    else:
        (o_ref,) = rest
    o_ref[0] = jnp.dot(hb, w_ref[...], preferred_element_type=F32)


def _norm_proj(x, mod, g, w, w_dt=None):
    bn, t, d = x.shape
    tm = min(t, TOKEN_TILE)
    n = w.shape[1]
    has_dt = w_dt is not None
    in_specs = [pl.BlockSpec((1, tm, d), lambda b, i: (b, i, 0)),
                _mod_spec(mod, tm, 0), _mod_spec(mod, tm, 1), _row_spec(d), _const_spec((d, n))]
    out_specs = [pl.BlockSpec((1, tm, n), lambda b, i: (b, i, 0))]
    out_shape = [jax.ShapeDtypeStruct((bn, t, n), F32)]
    args = [x, mod, mod, g.reshape(1, d), w]
    if has_dt:
        in_specs.append(_const_spec((d, 2 * LANES)))
        out_specs.append(pl.BlockSpec((1, tm, LANES), lambda b, i: (b, i, 0)))
        out_shape.append(jax.ShapeDtypeStruct((bn, t, LANES), F32))
        args.append(w_dt)
    return pl.pallas_call(
        functools.partial(_norm_proj_kernel, has_dt=has_dt),
        grid=(bn, t // tm),
        in_specs=in_specs, out_specs=out_specs, out_shape=out_shape,
        compiler_params=_cparams("parallel", "parallel"),
        name="norm_proj_dt" if has_dt else "norm_proj",
    )(*args)


def _in_even_kernel(x_ref, sh_ref, sc_ref, g_ref, w_ref, lb_ref, scw_ref, gla_ref, ob_ref, sct_ref,
                    tail, *, tm, slot):
    @pl.when(pl.program_id(1) == 0)
    def _():
        tail[...] = jnp.zeros_like(tail)

    hb = _modulated_norm(x_ref, sh_ref, sc_ref, g_ref).astype(BF16)
    proj = lambda j: jnp.dot(hb, w_ref[:, j * COL_BLOCK:(j + 1) * COL_BLOCK], preferred_element_type=F32)
    lb = _hgrn_lower_bound(lb_ref[...], slot)
    aw = A_WIDTH
    gla_ref[0, :, 0:aw] = proj(0)
    f = lb + (1.0 - lb) * jax.nn.sigmoid(proj(1))
    gla_ref[0, :, aw:2 * aw] = 1.0 - f
    gla_ref[0, :, 2 * aw:3 * aw] = jnp.log(f)
    gla_ref[0, :, 3 * aw:4 * aw] = proj(2)
    gla_ref[0, :, 4 * aw:5 * aw] = _silu(proj(3))
    bg = proj(4)
    u = proj(5) * proj(6)
    w = scw_ref[...]
    t8 = tail[...]
    conv = _shift_rows(u, t8, 2) * w[0:1] + _shift_rows(u, t8, 1) * w[1:2] + u * w[2:3]
    ob_ref[0] = (bg * conv).astype(BF16)
    tail[...] = u[tm - SUBLANES:tm]
    sct_ref[0] = u[tm - (SC_WIDTH - 1):tm]


def _in_even(x, mod, g, w, hgrn_lb, sc_w, slot):
    bn, t, d = x.shape
    tm = TOKEN_TILE
    n = w.shape[1]
    return pl.pallas_call(
        functools.partial(_in_even_kernel, tm=tm, slot=slot),
        grid=(bn, t // tm),
        in_specs=[pl.BlockSpec((1, tm, d), lambda b, i: (b, i, 0)),
                  _mod_spec(mod, tm, 0), _mod_spec(mod, tm, 1), _row_spec(d), _const_spec((d, n)),
                  pl.BlockSpec(hgrn_lb.shape, lambda b, i: (0, 0)),
                  pl.BlockSpec((SC_WIDTH, B_WIDTH), lambda b, i: (0, 0))],
        out_specs=[pl.BlockSpec((1, tm, 5 * A_WIDTH), lambda b, i: (b, i, 0)),
                   pl.BlockSpec((1, tm, B_WIDTH), lambda b, i: (b, i, 0)),
                   pl.BlockSpec((1, SC_WIDTH - 1, B_WIDTH), lambda b, i: (b, 0, 0))],
        out_shape=[jax.ShapeDtypeStruct((bn, t, 5 * A_WIDTH), F32),
                   jax.ShapeDtypeStruct((bn, t, B_WIDTH), BF16),
                   jax.ShapeDtypeStruct((bn, SC_WIDTH - 1, B_WIDTH), F32)],
        scratch_shapes=[pltpu.VMEM((SUBLANES, B_WIDTH), F32)],
        compiler_params=_cparams("parallel", "arbitrary"),
        name="in_even",
    )(x, mod, mod, g.reshape(1, d), w, hgrn_lb, sc_w)


def _in_odd_kernel(x_ref, sh_ref, sc_ref, g_ref, w_ref, wdt_ref, cw_ref, cb_ref, dtb_ref,
                   zs_ref, xs_ref, bc_ref, dt_ref, mct_ref, tail, *, tm):
    @pl.when(pl.program_id(1) == 0)
    def _():
        tail[...] = jnp.zeros_like(tail)

    h = _modulated_norm(x_ref, sh_ref, sc_ref, g_ref)
    hb = h.astype(BF16)
    dt_ref[0] = _softplus(_dt_proj(h, hb, wdt_ref) + dtb_ref[...])
    for j in range(M_INNER // COL_BLOCK):
        cols = slice(j * COL_BLOCK, (j + 1) * COL_BLOCK)
        zs_ref[0, :, cols] = _silu(jnp.dot(hb, w_ref[:, cols], preferred_element_type=F32))
    n_blocks = M_CONV_DIM // COL_BLOCK
    proj = lambda j: jnp.dot(hb, w_ref[:, M_INNER + j * COL_BLOCK:M_INNER + (j + 1) * COL_BLOCK],
                             preferred_element_type=F32)
    prev_tail = tail[...]
    new_tail = []
    xr_next = proj(0)
    for j in range(n_blocks):
        xr = xr_next
        if j + 1 < n_blocks:
            xr_next = proj(j + 1)
        cols = slice(j * COL_BLOCK, (j + 1) * COL_BLOCK)
        t8 = prev_tail[:, cols]
        cw = cw_ref[:, cols]
        conv = (_shift_rows(xr, t8, 3) * cw[0:1] + _shift_rows(xr, t8, 2) * cw[1:2]
                + _shift_rows(xr, t8, 1) * cw[2:3] + xr * cw[3:4])
        new_tail.append(xr[tm - SUBLANES:tm])
        xbc = _silu(conv + cb_ref[:, cols])
        if j < M_INNER // COL_BLOCK:
            xs_ref[0, :, cols] = xbc
        else:
            bc_ref[0, :, j * COL_BLOCK - M_INNER:(j + 1) * COL_BLOCK - M_INNER] = xbc.astype(BF16)
    new_tail = jnp.concatenate(new_tail, axis=1)
    tail[...] = new_tail
    mct_ref[0] = new_tail[SUBLANES - (M_CONV - 1):]


def _in_odd(x, mod, g, w, w_dt, conv_w, conv_b, dtb):
    bn, t, d = x.shape
    tm = TOKEN_TILE
    n = w.shape[1]
    tok = lambda width: pl.BlockSpec((1, tm, width), lambda b, i: (b, i, 0))
    return pl.pallas_call(
        functools.partial(_in_odd_kernel, tm=tm),
        grid=(bn, t // tm),
        in_specs=[tok(d), _mod_spec(mod, tm, 0), _mod_spec(mod, tm, 1), _row_spec(d),
                  _const_spec((d, n)), _const_spec((d, 2 * LANES)),
                  pl.BlockSpec((M_CONV, M_CONV_DIM), lambda b, i: (0, 0)),
                  _row_spec(M_CONV_DIM), _row_spec(LANES)],
        out_specs=[tok(M_INNER), tok(M_INNER), tok(2 * M_GN), tok(LANES),
                   pl.BlockSpec((1, M_CONV - 1, M_CONV_DIM), lambda b, i: (b, 0, 0))],
        out_shape=[jax.ShapeDtypeStruct((bn, t, M_INNER), F32),
                   jax.ShapeDtypeStruct((bn, t, M_INNER), F32),
                   jax.ShapeDtypeStruct((bn, t, 2 * M_GN), BF16),
                   jax.ShapeDtypeStruct((bn, t, LANES), F32),
                   jax.ShapeDtypeStruct((bn, M_CONV - 1, M_CONV_DIM), F32)],
        scratch_shapes=[pltpu.VMEM((SUBLANES, M_CONV_DIM), F32)],
        compiler_params=_cparams("parallel", "arbitrary"),
        name="in_odd",
    )(x, mod, mod, g.reshape(1, d), w, w_dt, conv_w, conv_b, dtb)


def _post_kernel(*refs, n_o, final, fc):
    x_ref = refs[0]
    o_refs = refs[1:1 + n_o]
    g1_ref, sh2_ref, s2_ref, g2_ref, nm_ref = refs[1 + n_o:6 + n_o]
    wo_refs = refs[6 + n_o:6 + 2 * n_o]
    w1_ref, w2_ref = refs[6 + 2 * n_o:8 + 2 * n_o]
    rest = refs[8 + 2 * n_o:]
    mix = sum(jnp.dot(o[0].astype(BF16), wo[...], preferred_element_type=F32)
              for o, wo in zip(o_refs, wo_refs))
    x1 = x_ref[0] + g1_ref[0] * mix
    h = ((_rms(x1) * nm_ref[...]) * (1.0 + s2_ref[0]) + sh2_ref[0]).astype(BF16)
    acc = jnp.zeros_like(x1)
    for c in range(D_FF // fc):
        a = jnp.dot(h, w1_ref[:, c * fc:(c + 1) * fc], preferred_element_type=F32)
        a = jnp.square(jnp.maximum(a, 0.0)).astype(BF16)
        acc = acc + jnp.dot(a, w2_ref[c * fc:(c + 1) * fc, :], preferred_element_type=F32)
    x2 = x1 + g2_ref[0] * acc
    if final:
        nf_ref, out_ref = rest
        out_ref[0] = _rms(x2) * nf_ref[...]
    else:
        (out_ref,) = rest
        out_ref[0] = x2


def _post(x, os_, mod, nm, wos, w1, w2, nf=None):
    bn, t, d = x.shape
    tm = min(t, TOKEN_TILE)
    final = nf is not None
    in_specs = [pl.BlockSpec((1, tm, d), lambda b, i: (b, i, 0))]
    in_specs += [pl.BlockSpec((1, tm, o.shape[2]), lambda b, i: (b, i, 0)) for o in os_]
    in_specs += [_mod_spec(mod, tm, c) for c in (2, 3, 4, 5)] + [_row_spec(d)]
    in_specs += [_const_spec(wo.shape) for wo in wos] + [_const_spec((d, D_FF)), _const_spec((D_FF, d))]
    args = [x, *os_, mod, mod, mod, mod, nm.reshape(1, d), *wos, w1, w2]
    if final:
        in_specs.append(_row_spec(d))
        args.append(nf.reshape(1, d))
    return pl.pallas_call(
        functools.partial(_post_kernel, n_o=len(os_), final=final, fc=1024),
        grid=(bn, t // tm),
        in_specs=in_specs,
        out_specs=pl.BlockSpec((1, tm, d), lambda b, i: (b, i, 0)),
        out_shape=jax.ShapeDtypeStruct((bn, t, d), F32),
        compiler_params=_cparams("parallel", "parallel"),
        name="post_final" if final else "post",
    )(*args)


def _gla_kernel(g_ref, gn_ref, o_ref, s_ref, st_scr, *, tb):
    t = pl.program_id(1)

    @pl.when(t == 0)
    def _():
        st_scr[...] = jnp.zeros_like(st_scr)

    mask = _seg_causal(tb, GLA_CHUNK)
    tri = jnp.where(mask, 1.0, 0.0).astype(BF16)
    gn = gn_ref[...]
    n_chunks = tb // GLA_CHUNK
    for h in range(A_HEADS):
        part = lambda p: g_ref[0, :, p * A_WIDTH + h * A_DK:p * A_WIDTH + (h + 1) * A_DK]
        q, k, lf, v, sg = part(0), part(1), part(2), part(3), part(4)
        b = _dot3(tri, lf)
        qd = q * jnp.exp(b)
        kd = k * jnp.exp(-b)
        o = _bdot(jnp.where(mask, _bdot_nt(qd, kd), 0.0), v)
        kcols, decs = [], []
        for n in range(n_chunks):
            rows = slice(n * GLA_CHUNK, (n + 1) * GLA_CHUNK)
            bl = b[(n + 1) * GLA_CHUNK - 1:(n + 1) * GLA_CHUNK]
            kdec = (k[rows] * jnp.exp(bl - b[rows])).astype(BF16)
            kcols.append(jnp.concatenate(
                ([jnp.zeros((n * GLA_CHUNK, A_DK), BF16)] if n else []) + [kdec]
                + ([jnp.zeros((tb - (n + 1) * GLA_CHUNK, A_DK), BF16)] if n < n_chunks - 1 else []), axis=0))
            decs.append(jnp.exp(bl))
        ds = _bdot(v.T, jnp.concatenate(kcols, axis=1))
        st = st_scr[h]
        inter = []
        for n in range(n_chunks):
            inter.append(_bdot_nt(qd[n * GLA_CHUNK:(n + 1) * GLA_CHUNK], st))
            st = st * decs[n] + ds[:, n * A_DK:(n + 1) * A_DK]
        st_scr[h] = st
        o = o + jnp.concatenate(inter, axis=0)
        o_ref[0, :, h * A_DK:(h + 1) * A_DK] = ((_rms(o) * gn) * sg).astype(BF16)

    @pl.when(t == pl.num_programs(1) - 1)
    def _():
        for h in range(A_HEADS):
            s_ref[0, h] = st_scr[h].T


def _gla(gla_in, gnorm):
    bn, t, n = gla_in.shape
    tb = GLA_TILE
    return pl.pallas_call(
        functools.partial(_gla_kernel, tb=tb),
        grid=(bn, t // tb),
        in_specs=[pl.BlockSpec((1, tb, n), lambda b, i: (b, i, 0)), _row_spec(A_DK)],
        out_specs=[pl.BlockSpec((1, tb, A_WIDTH), lambda b, i: (b, i, 0)),
                   pl.BlockSpec((1, A_HEADS, A_DK, A_DK), lambda b, i: (b, 0, 0, 0))],
        out_shape=[jax.ShapeDtypeStruct((bn, t, A_WIDTH), BF16),
                   jax.ShapeDtypeStruct((bn, A_HEADS, A_DK, A_DK), F32)],
        scratch_shapes=[pltpu.VMEM((A_HEADS, A_DK, A_DK), F32)],
        compiler_params=_cparams("parallel", "arbitrary"),
        name="gla",
    )(gla_in, gnorm.reshape(1, A_DK))


def _even_sample_kernel(proj_ref, s_ref, scb_ref, lb_ref, gn_ref, scw_ref, o_ref, so_ref, sco_ref,
                        *, tt, slot):
    lb = _hgrn_lower_bound(lb_ref[...], slot)
    gn = gn_ref[...]
    for h in range(A_HEADS):
        cols = lambda part: slice(part * A_WIDTH + h * A_DK, part * A_WIDTH + (h + 1) * A_DK)
        q = proj_ref[:, cols(0)]
        fz = proj_ref[:, cols(1)]
        v = proj_ref[:, cols(2)]
        go = proj_ref[:, cols(3)]
        lbh = lb[:, h * A_DK:(h + 1) * A_DK]
        f = lbh + (1.0 - lbh) * jax.nn.sigmoid(fz)
        f_t = _pad_t(f)
        q_t = _pad_t(q)
        outs = []
        for i in range(tt):
            fc = f_t[:, i:i + 1]
            s_new = fc * s_ref[i, h] + (1.0 - fc) * v[i:i + 1, :]
            so_ref[i, h] = s_new
            outs.append(jnp.sum(q_t[:, i:i + 1] * s_new, axis=0, keepdims=True))
        o = jnp.concatenate(outs, axis=0)
        o_ref[:, h * A_DK:(h + 1) * A_DK] = (_rms(o) * gn) * _silu(go)

    off = 4 * A_WIDTH
    bg = proj_ref[:, off:off + B_WIDTH]
    u = proj_ref[:, off + B_WIDTH:off + 2 * B_WIDTH] * proj_ref[:, off + 2 * B_WIDTH:off + 3 * B_WIDTH]
    w = scw_ref[...]
    conv = scb_ref[:, 0:B_WIDTH] * w[0:1] + scb_ref[:, B_WIDTH:2 * B_WIDTH] * w[1:2] + u * w[2:3]
    o_ref[:, A_WIDTH:A_WIDTH + B_WIDTH] = bg * conv
    sco_ref[:, 0:B_WIDTH] = scb_ref[:, B_WIDTH:2 * B_WIDTH]
    sco_ref[:, B_WIDTH:2 * B_WIDTH] = u


def _even_sample(proj, state, sc_buf, hgrn_lb, gnorm, sc_w, slot):
    bs, n = proj.shape
    tt = SUBLANES
    nb = sc_buf.shape[1]
    return pl.pallas_call(
        functools.partial(_even_sample_kernel, tt=tt, slot=slot),
        grid=(bs // tt,),
        in_specs=[pl.BlockSpec((tt, n), lambda i: (i, 0)),
                  pl.BlockSpec((tt, A_HEADS, A_DK, A_DK), lambda i: (i, 0, 0, 0)),
                  pl.BlockSpec((tt, nb), lambda i: (i, 0)),
                  pl.BlockSpec(hgrn_lb.shape, lambda i: (0, 0)),
                  pl.BlockSpec((1, A_DK), lambda i: (0, 0)),
                  pl.BlockSpec((SC_WIDTH, B_WIDTH), lambda i: (0, 0))],
        out_specs=[pl.BlockSpec((tt, D_MODEL), lambda i: (i, 0)),
                   pl.BlockSpec((tt, A_HEADS, A_DK, A_DK), lambda i: (i, 0, 0, 0)),
                   pl.BlockSpec((tt, nb), lambda i: (i, 0))],
        out_shape=[jax.ShapeDtypeStruct((bs, D_MODEL), F32),
                   jax.ShapeDtypeStruct(state.shape, F32),
                   jax.ShapeDtypeStruct(sc_buf.shape, F32)],
        compiler_params=_cparams("parallel"),
        name="even_sample",
    )(proj, state, sc_buf, hgrn_lb, gnorm.reshape(1, A_DK), sc_w)


def _pair_cols(m, lo, h0):
    return jnp.where(lo, m[:, h0:h0 + 1], m[:, h0 + 1:h0 + 2])


def _ssd_kernel(xs_ref, bc_ref, zs_ref, dt_ref, alog_ref, dsk_ref, nw_ref, y_ref, s_ref, st_scr):
    t = pl.program_id(1)
    q = SSD_CHUNK

    @pl.when(t == 0)
    def _():
        st_scr[...] = jnp.zeros_like(st_scr)

    dt = dt_ref[0]
    d_a = dt * (-jnp.exp(alog_ref[...]))
    r = lax.broadcasted_iota(jnp.int32, (q, q), 0)
    c = lax.broadcasted_iota(jnp.int32, (q, q), 1)
    causal = c <= r
    cs = _dot3(jnp.where(causal, 1.0, 0.0).astype(BF16), d_a)
    cs_t = cs.T
    dt_t = dt.T
    ecs = jnp.exp(cs)
    cs_last = cs[q - 1:q, :]
    wx = jnp.exp(cs_last - cs) * dt
    dec = jnp.exp(cs_last)
    lo = c < M_HEADDIM
    lo1 = lo[0:1]
    gw = M_INNER // M_GROUPS
    for g in range(M_GROUPS):
        bm = bc_ref[0, :, g * M_STATE:(g + 1) * M_STATE]
        cm = bc_ref[0, :, M_GN + g * M_STATE:M_GN + (g + 1) * M_STATE]
        cb = _bdot_nt(cm, bm)
        st_g = st_scr[:, g * gw:(g + 1) * gw]
        y_off = _bdot(cm, st_g)
        ys, xws, decs = [], [], []
        for p in range(gw // LANES):
            m = g * (gw // LANES) + p
            h0 = 2 * m
            xp = xs_ref[0, :, m * LANES:(m + 1) * LANES]
            yd = []
            for h in (h0, h0 + 1):
                seg = jnp.minimum(cs[:, h:h + 1] - cs_t[h:h + 1, :], 0.0)
                w = cb * jnp.where(causal, jnp.exp(seg), 0.0) * dt_t[h:h + 1, :]
                yd.append(_bdot(w, xp))
            y = jnp.where(lo, yd[0], yd[1])
            y = y + y_off[:, p * LANES:(p + 1) * LANES] * _pair_cols(ecs, lo, h0)
            ys.append(y + dsk_ref[:, m * LANES:(m + 1) * LANES] * xp)
            xws.append(xp * _pair_cols(wx, lo, h0))
            decs.append(_pair_cols(dec, lo1, h0))
        xw = jnp.concatenate(xws, axis=1)
        st_scr[:, g * gw:(g + 1) * gw] = (st_g * jnp.concatenate(decs, axis=1)
                                          + _bdot(bm.astype(F32).T, xw))
        y = jnp.concatenate(ys, axis=1) * zs_ref[0, :, g * gw:(g + 1) * gw]
        y_ref[0, :, g * gw:(g + 1) * gw] = (_rms(y) * nw_ref[:, g * gw:(g + 1) * gw]).astype(BF16)

    @pl.when(t == pl.num_programs(1) - 1)
    def _():
        for m in range(M_INNER // LANES):
            blk = st_scr[:, m * LANES:(m + 1) * LANES].T
            s_ref[0, 2 * m] = blk[0:M_HEADDIM]
            s_ref[0, 2 * m + 1] = blk[M_HEADDIM:2 * M_HEADDIM]


def _ssd(xs, bc, zs, dt, alog, dsk, nw):
    bn, t, _ = xs.shape
    q = SSD_CHUNK
    tok = lambda width: pl.BlockSpec((1, q, width), lambda b, i: (b, i, 0))
    return pl.pallas_call(
        _ssd_kernel,
        grid=(bn, t // q),
        in_specs=[tok(M_INNER), tok(2 * M_GN), tok(M_INNER), tok(LANES),
                  _row_spec(LANES), _row_spec(M_INNER), _row_spec(M_INNER)],
        out_specs=[tok(M_INNER),
                   pl.BlockSpec((1, M_HEADS, M_HEADDIM, M_STATE), lambda b, i: (b, 0, 0, 0))],
        out_shape=[jax.ShapeDtypeStruct((bn, t, M_INNER), BF16),
                   jax.ShapeDtypeStruct((bn, M_HEADS, M_HEADDIM, M_STATE), F32)],
        scratch_shapes=[pltpu.VMEM((M_STATE, M_INNER), F32)],
        compiler_params=_cparams("parallel", "arbitrary"),
        name="ssd",
    )(xs, bc, zs, dt, alog, dsk, nw)


def _mconv_sample_kernel(zx_ref, dtr_ref, buf_ref, cw_ref, cb_ref, dtb_ref, alog_ref,
                         xbc_ref, nbuf_ref, dte_ref, decb_ref):
    n = M_CONV_DIM
    u = zx_ref[:, M_INNER:M_INNER + n]
    cw = cw_ref[...]
    conv = (buf_ref[:, 0:n] * cw[0:1] + buf_ref[:, n:2 * n] * cw[1:2]
            + buf_ref[:, 2 * n:3 * n] * cw[2:3] + u * cw[3:4])
    xbc_ref[...] = _silu(conv + cb_ref[...])
    nbuf_ref[:, 0:2 * n] = buf_ref[:, n:3 * n]
    nbuf_ref[:, 2 * n:3 * n] = u
    dt = _softplus(dtr_ref[...] + dtb_ref[...])
    dec = jnp.exp(dt * (-jnp.exp(alog_ref[...])))
    tt = dt.shape[0]
    lo = lax.broadcasted_iota(jnp.int32, (tt, LANES), 1) < M_HEADDIM
    for m in range(M_INNER // LANES):
        dte_ref[:, m * LANES:(m + 1) * LANES] = _pair_cols(dt, lo, 2 * m)
    for h in range(M_HEADS):
        decb_ref[h] = jnp.broadcast_to(dec[:, h:h + 1], (tt, LANES))


def _mconv_sample(zx, dtr, buf, conv_w, conv_b, dtb, alog):
    bs, n = zx.shape
    tt = 32
    nb = buf.shape[1]
    row = lambda w: pl.BlockSpec((1, w), lambda i: (0, 0))
    tok = lambda w: pl.BlockSpec((tt, w), lambda i: (i, 0))
    return pl.pallas_call(
        _mconv_sample_kernel,
        grid=(bs // tt,),
        in_specs=[tok(n), tok(LANES), tok(nb),
                  pl.BlockSpec((M_CONV, M_CONV_DIM), lambda i: (0, 0)),
                  row(M_CONV_DIM), row(LANES), row(LANES)],
        out_specs=[tok(M_CONV_DIM), tok(nb), tok(M_INNER),
                   pl.BlockSpec((M_HEADS, tt, LANES), lambda i: (0, i, 0))],
        out_shape=[jax.ShapeDtypeStruct((bs, M_CONV_DIM), F32),
                   jax.ShapeDtypeStruct((bs, nb), F32),
                   jax.ShapeDtypeStruct((bs, M_INNER), F32),
                   jax.ShapeDtypeStruct((M_HEADS, bs, LANES), F32)],
        compiler_params=_cparams("parallel"),
        name="mconv_sample",
    )(zx, dtr, buf, conv_w, conv_b, dtb, alog)


def _ssd_sample_kernel(x_ref, b_ref, c_ref, z_ref, dte_ref, decb_ref, s_ref, dsk_ref, nw_ref,
                       y_ref, so_ref, *, tt):
    x = x_ref[...]
    gw = x.shape[1]
    hpg = gw // M_HEADDIM
    xh, xm, xl = _split3(x * dte_ref[...])
    bh, bmid, bl = _split3(b_ref[...])
    n_terms = 6
    lhs_t = jnp.concatenate([xh, xh, xm, xh, xm, xl, jnp.zeros((LANES - n_terms * tt, gw), F32)], axis=0)
    lhs = jnp.concatenate([lhs_t[:, p * LANES:(p + 1) * LANES].T for p in range(gw // LANES)],
                          axis=0).astype(BF16)
    rhs_all = jnp.concatenate([bh, bmid, bh, bl, bmid, bh,
                               jnp.zeros((LANES - n_terms * tt, M_STATE), F32)], axis=0)
    row_tok = lax.broadcasted_iota(jnp.int32, (LANES, M_STATE), 0) & (tt - 1)
    cmb = c_ref[...].astype(BF16)
    out_row = lax.broadcasted_iota(jnp.int32, (tt, gw), 0)
    y = jnp.zeros((tt, gw), F32)
    for i in range(tt):
        rhs = jnp.where(row_tok == i, rhs_all, 0.0).astype(BF16)
        xb = jnp.dot(lhs, rhs, preferred_element_type=F32)
        pieces = []
        for hh in range(hpg):
            rows = slice(hh * M_HEADDIM, (hh + 1) * M_HEADDIM)
            s_new = decb_ref[hh, i:i + 1, :] * s_ref[i, hh] + xb[rows]
            so_ref[i, hh] = s_new
            pieces.append(s_new.astype(BF16))
        yi = lax.dot_general(cmb, jnp.concatenate(pieces, axis=0), (((1,), (1,)), ((), ())),
                             preferred_element_type=F32)
        y = jnp.where(out_row == i, yi, y)
    y = (y + dsk_ref[...] * x) * _silu(z_ref[...])
    y_ref[...] = _rms(y) * nw_ref[...]


def _ssd_sample(xbc, zx, dte, decb, state, dsk, nw):
    bs = xbc.shape[0]
    tt = SUBLANES
    gw = M_INNER // M_GROUPS
    hpg = M_HEADS // M_GROUPS
    return pl.pallas_call(
        functools.partial(_ssd_sample_kernel, tt=tt),
        grid=(bs // tt, M_GROUPS),
        in_specs=[pl.BlockSpec((tt, gw), lambda i, g: (i, g)),
                  pl.BlockSpec((tt, M_STATE), lambda i, g: (i, M_INNER // M_STATE + g)),
                  pl.BlockSpec((tt, M_STATE), lambda i, g: (i, (M_INNER + M_GN) // M_STATE + g)),
                  pl.BlockSpec((tt, gw), lambda i, g: (i, g)),
                  pl.BlockSpec((tt, gw), lambda i, g: (i, g)),
                  pl.BlockSpec((hpg, tt, LANES), lambda i, g: (g, i, 0)),
                  pl.BlockSpec((tt, hpg, M_HEADDIM, M_STATE), lambda i, g: (i, g, 0, 0)),
                  pl.BlockSpec((1, gw), lambda i, g: (0, g)),
                  pl.BlockSpec((1, gw), lambda i, g: (0, g))],
        out_specs=[pl.BlockSpec((tt, gw), lambda i, g: (i, g)),
                   pl.BlockSpec((tt, hpg, M_HEADDIM, M_STATE), lambda i, g: (i, g, 0, 0))],
        out_shape=[jax.ShapeDtypeStruct((bs, M_INNER), F32),
                   jax.ShapeDtypeStruct(state.shape, F32)],
        compiler_params=_cparams("parallel", "parallel"),
        name="ssd_sample",
    )(xbc, xbc, xbc, zx, dte, decb, state, dsk, nw)


def _pad_lanes(v):
    return jnp.pad(v.reshape(1, -1), ((0, 0), (0, LANES - v.shape[-1])))


def kernel(x_prompt, x_sample, c_prompt, c_sample, state_hgrn, state_shortconv, state_ssm, state_mconv, ada_w, ada_b, norm_mix, norm_mlp, norm_final, w_in_even, hgrn_lb, hgrn_gnorm, sc_w, w_out_even, w_in_odd, mconv_w, mconv_b, dt_bias, a_log, d_skip, m_norm, w_out_odd, mlp_w1, mlp_w2):
    bp = x_prompt.shape[0]
    bs = x_sample.shape[0]
    mod = _adaln_mod(jnp.concatenate([c_prompt, c_sample], axis=0), ada_w, ada_b)
    mod_p = [mod[l, :bp].reshape(bp, 1, -1) for l in range(2)]
    mod_s = [mod[l, bp:].reshape(1, bs, -1) for l in range(2)]

    w_even = w_in_even[0].astype(BF16)
    wo_even = w_out_even[0].astype(BF16)
    nzx = M_INNER + M_CONV_DIM
    w_zx = w_in_odd[0][:, :nzx].astype(BF16)
    w_dt = jnp.pad(w_in_odd[0][:, nzx:], ((0, 0), (0, LANES - M_HEADS)))
    w_dt_hi = w_dt.astype(BF16)
    w_dt = jnp.concatenate([w_dt_hi, (w_dt - w_dt_hi.astype(F32)).astype(BF16)], axis=1)
    wo_odd = w_out_odd[0].astype(BF16)
    w1 = mlp_w1.astype(BF16)
    w2 = mlp_w2.astype(BF16)
    dtb = _pad_lanes(dt_bias[0])
    alog = _pad_lanes(a_log[0])
    dsk = jnp.repeat(d_skip[0], M_HEADDIM).reshape(1, M_INNER)
    nw = m_norm[0].reshape(1, M_INNER)
    cb = mconv_b[0].reshape(1, M_CONV_DIM)

    gla_in, o_b, sc_p = _in_even(x_prompt, mod_p[0], norm_mix[0], w_even, hgrn_lb, sc_w[0], 1)
    o_a, hg_p = _gla(gla_in, hgrn_gnorm[0])
    x1 = _post(x_prompt, [o_a, o_b], mod_p[0], norm_mlp[0], [wo_even[:A_WIDTH], wo_even[A_WIDTH:]],
               w1[0], w2[0])
    zs, xs_c, bc, dt, mc_p = _in_odd(x1, mod_p[1], norm_mix[1], w_zx, w_dt, mconv_w[0], cb, dtb)
    y, ssm_p = _ssd(xs_c, bc, zs, dt, alog, dsk, nw)
    y_prompt = _post(x1, [y], mod_p[1], norm_mlp[1], [wo_odd], w1[1], w2[1], norm_final)

    xs = x_sample.reshape(1, bs, D_MODEL)
    (proj_s,) = _norm_proj(xs, mod_s[0], norm_mix[0], w_even)
    o_s, hg_s, sc_s = _even_sample(proj_s[0], state_hgrn[0], state_shortconv[0].reshape(bs, -1),
                                   hgrn_lb, hgrn_gnorm[0], sc_w[0], 1)
    x1s = _post(xs, [o_s[None]], mod_s[0], norm_mlp[0], [wo_even], w1[0], w2[0])
    zx_s, dtr_s = _norm_proj(x1s, mod_s[1], norm_mix[1], w_zx, w_dt)
    xbc_s, mc_s, dte, decb = _mconv_sample(zx_s[0], dtr_s[0], state_mconv[0].reshape(bs, -1),
                                           mconv_w[0], cb, dtb, alog)
    y_s, ssm_s = _ssd_sample(xbc_s, zx_s[0], dte, decb, state_ssm[0], dsk, nw)
    y_sample = _post(x1s, [y_s[None]], mod_s[1], norm_mlp[1], [wo_odd], w1[1], w2[1], norm_final)

    return (y_prompt, y_sample.reshape(bs, 1, D_MODEL),
            hg_p[None], hg_s[None],
            sc_p[None], sc_s.reshape(1, bs, SC_WIDTH - 1, B_WIDTH),
            ssm_p[None], ssm_s[None],
            mc_p[None], mc_s.reshape(1, bs, M_CONV - 1, M_CONV_DIM))
```

```python
import functools

import jax
import jax.numpy as jnp
from jax import lax
from jax.experimental import pallas as pl
from jax.experimental.pallas import tpu as pltpu

F32 = jnp.float32
BF16 = jnp.bfloat16
EPS = 1e-6

D_MODEL = 1024
A_HEADS = 4
A_DK = 128
A_WIDTH = 512
B_WIDTH = 512
M_INNER = 2048
M_HEADDIM = 64
M_HEADS = 32
M_STATE = 128
M_GROUPS = 4
M_GN = M_GROUPS * M_STATE
M_CONV_DIM = M_INNER + 2 * M_GN
M_CONV = 4
SC_WIDTH = 3
D_FF = 4096

LANES = 128
SUBLANES = 8
GLA_CHUNK = 32
SSD_CHUNK = 128
V7X_VMEM_BYTES = 64 * 1024 * 1024
VMEM_LIMIT = V7X_VMEM_BYTES - 8 * 1024 * 1024
TOKEN_TILE = 512
GLA_TILE = 256
GLA_STEP = 512
SSD_TILE = 512
COL_BLOCK = 512


def _cparams(*sem):
    return pltpu.CompilerParams(dimension_semantics=sem, vmem_limit_bytes=VMEM_LIMIT)


def _rms(x):
    return x * lax.rsqrt(jnp.mean(x * x, axis=-1, keepdims=True) + EPS)


def _silu(x):
    return x * jax.nn.sigmoid(x)


def _softplus(x):
    return jnp.maximum(x, 0.0) + jnp.log1p(jnp.exp(-jnp.abs(x)))


def _bdot(a, b):
    return jnp.dot(a.astype(BF16), b.astype(BF16), preferred_element_type=F32)


def _bdot_nt(a, b):
    return lax.dot_general(a.astype(BF16), b.astype(BF16), (((1,), (1,)), ((), ())),
                           preferred_element_type=F32)


def _dot3(m01, x):
    hi = x.astype(BF16)
    r = x - hi.astype(F32)
    mid = r.astype(BF16)
    lo = (r - mid.astype(F32)).astype(BF16)
    d = lambda v: jnp.dot(m01, v, preferred_element_type=F32)
    return d(hi) + d(mid) + d(lo)


def _split3(x):
    hi = x.astype(BF16).astype(F32)
    r = x - hi
    mid = r.astype(BF16).astype(F32)
    return hi, mid, (r - mid).astype(BF16).astype(F32)


def _seg_causal(n, seg):
    r = lax.broadcasted_iota(jnp.int32, (n, n), 0)
    c = lax.broadcasted_iota(jnp.int32, (n, n), 1)
    shift = seg.bit_length() - 1
    return jnp.logical_and(c <= r, c >= ((r >> shift) << shift))


def _pad_t(a):
    r = a.shape[0]
    return jnp.concatenate([a, jnp.zeros((LANES - r, LANES), F32)], axis=0).T


def _hgrn_lower_bound(lb_all, slot):
    m = jnp.max(lb_all, axis=0, keepdims=True)
    e = jnp.exp(lb_all - m)
    p = e / jnp.sum(e, axis=0, keepdims=True)
    return jnp.sum(p[1:slot + 1], axis=0, keepdims=True)


def _shift_rows(u, tail, s):
    rolled = pltpu.roll(u, s, axis=0)
    row = lax.broadcasted_iota(jnp.int32, tail.shape, 0)
    head = jnp.where(row < s, pltpu.roll(tail, s, axis=0), rolled[0:SUBLANES])
    return jnp.concatenate([head, rolled[SUBLANES:]], axis=0)


def _modulated_norm(x_ref, sh_ref, sc_ref, g_ref):
    return (_rms(x_ref[0]) * g_ref[...]) * (1.0 + sc_ref[0]) + sh_ref[0]


def _dt_proj(h, hb, wdt_ref):
    h_lo = (h - hb.astype(F32)).astype(BF16)
    a = jnp.dot(hb, wdt_ref[...], preferred_element_type=F32)
    return a[:, :LANES] + a[:, LANES:] + jnp.dot(h_lo, wdt_ref[:, :LANES], preferred_element_type=F32)


def _mod_kernel(c_ref, w_ref, b_ref, o_ref):
    c = c_ref[...]
    a = _silu(c).astype(BF16)
    o_ref[0] = jnp.dot(a, w_ref[0].astype(BF16), preferred_element_type=F32) + b_ref[0]


def _adaln_mod(c_all, ada_w, ada_b):
    n_layers, d, n = ada_w.shape
    r = c_all.shape[0]
    tn = 1536
    return pl.pallas_call(
        _mod_kernel,
        grid=(n_layers, n // tn),
        in_specs=[pl.BlockSpec((r, d), lambda l, j: (0, 0)),
                  pl.BlockSpec((1, d, tn), lambda l, j: (l, 0, j)),
                  pl.BlockSpec((1, 1, tn), lambda l, j: (l, 0, j))],
        out_specs=pl.BlockSpec((1, r, tn), lambda l, j: (l, 0, j)),
        out_shape=jax.ShapeDtypeStruct((n_layers, r, n), F32),
        compiler_params=_cparams("parallel", "parallel"),
        name="adaln_mod",
    )(c_all, ada_w, ada_b.reshape(n_layers, 1, n))


def _mod_spec(mod, tm, col):
    if mod.shape[1] == 1:
        return pl.BlockSpec((1, 1, D_MODEL), lambda b, t: (b, 0, col))
    return pl.BlockSpec((1, tm, D_MODEL), lambda b, t: (b, t, col))


def _const_spec(shape):
    return pl.BlockSpec(shape, lambda b, i: (0,) * len(shape), pipeline_mode=pl.Buffered(1))


def _row_spec(w):
    return pl.BlockSpec((1, w), lambda b, i: (0, 0))


def _norm_proj_kernel(x_ref, sh_ref, sc_ref, g_ref, w_ref, *rest, n, has_dt):
    h = _modulated_norm(x_ref, sh_ref, sc_ref, g_ref)
    hb = h.astype(BF16)
    if has_dt:
        wdt_ref, o_ref, odt_ref = rest
        odt_ref[0] = _dt_proj(h, hb, wdt_ref)
    else:
        (o_ref,) = rest
    o_ref[0] = jnp.dot(hb, w_ref[:, :n], preferred_element_type=F32)


def _norm_proj(x, mod, g, w, n, w_dt=None):
    bn, t, d = x.shape
    tm = min(t, TOKEN_TILE)
    has_dt = w_dt is not None
    in_specs = [pl.BlockSpec((1, tm, d), lambda b, i: (b, i, 0)),
                _mod_spec(mod, tm, 0), _mod_spec(mod, tm, 1), _row_spec(d), _const_spec(w.shape)]
    out_specs = [pl.BlockSpec((1, tm, n), lambda b, i: (b, i, 0))]
    out_shape = [jax.ShapeDtypeStruct((bn, t, n), F32)]
    args = [x, mod, mod, g.reshape(1, d), w]
    if has_dt:
        in_specs.append(_const_spec((d, 2 * LANES)))
        out_specs.append(pl.BlockSpec((1, tm, LANES), lambda b, i: (b, i, 0)))
        out_shape.append(jax.ShapeDtypeStruct((bn, t, LANES), F32))
        args.append(w_dt)
    return pl.pallas_call(
        functools.partial(_norm_proj_kernel, n=n, has_dt=has_dt),
        grid=(bn, t // tm),
        in_specs=in_specs, out_specs=out_specs, out_shape=out_shape,
        compiler_params=_cparams("parallel", "parallel"),
        name="norm_proj_dt" if has_dt else "norm_proj",
    )(*args)


def _in_even_kernel(x_ref, sh_ref, sc_ref, g_ref, w_ref, lb_ref, scw_ref, gla_ref, ob_ref, sct_ref,
                    tail, *, tm, slot):
    @pl.when(pl.program_id(1) == 0)
    def _():
        tail[...] = jnp.zeros_like(tail)

    hb = _modulated_norm(x_ref, sh_ref, sc_ref, g_ref).astype(BF16)
    proj = lambda j: jnp.dot(hb, w_ref[:, j * COL_BLOCK:(j + 1) * COL_BLOCK], preferred_element_type=F32)
    lb = _hgrn_lower_bound(lb_ref[...], slot)
    aw = A_WIDTH
    gla_ref[0, :, 0:aw] = proj(0)
    f = lb + (1.0 - lb) * jax.nn.sigmoid(proj(1))
    gla_ref[0, :, aw:2 * aw] = 1.0 - f
    gla_ref[0, :, 2 * aw:3 * aw] = jnp.log(f)
    gla_ref[0, :, 3 * aw:4 * aw] = proj(2)
    gla_ref[0, :, 4 * aw:5 * aw] = _silu(proj(3))
    bg = proj(4)
    u = proj(5) * proj(6)
    w = scw_ref[...]
    t8 = tail[...]
    conv = _shift_rows(u, t8, 2) * w[0:1] + _shift_rows(u, t8, 1) * w[1:2] + u * w[2:3]
    ob_ref[0] = (bg * conv).astype(BF16)
    tail[...] = u[tm - SUBLANES:tm]
    sct_ref[0] = u[tm - (SC_WIDTH - 1):tm]


def _in_even(x, mod, g, w, hgrn_lb, sc_w, slot):
    bn, t, d = x.shape
    tm = TOKEN_TILE
    n = w.shape[1]
    return pl.pallas_call(
        functools.partial(_in_even_kernel, tm=tm, slot=slot),
        grid=(bn, t // tm),
        in_specs=[pl.BlockSpec((1, tm, d), lambda b, i: (b, i, 0)),
                  _mod_spec(mod, tm, 0), _mod_spec(mod, tm, 1), _row_spec(d), _const_spec((d, n)),
                  pl.BlockSpec(hgrn_lb.shape, lambda b, i: (0, 0)),
                  pl.BlockSpec((SC_WIDTH, B_WIDTH), lambda b, i: (0, 0))],
        out_specs=[pl.BlockSpec((1, tm, 5 * A_WIDTH), lambda b, i: (b, i, 0)),
                   pl.BlockSpec((1, tm, B_WIDTH), lambda b, i: (b, i, 0)),
                   pl.BlockSpec((1, SC_WIDTH - 1, B_WIDTH), lambda b, i: (b, 0, 0))],
        out_shape=[jax.ShapeDtypeStruct((bn, t, 5 * A_WIDTH), F32),
                   jax.ShapeDtypeStruct((bn, t, B_WIDTH), BF16),
                   jax.ShapeDtypeStruct((bn, SC_WIDTH - 1, B_WIDTH), F32)],
        scratch_shapes=[pltpu.VMEM((SUBLANES, B_WIDTH), F32)],
        compiler_params=_cparams("parallel", "arbitrary"),
        name="in_even",
    )(x, mod, mod, g.reshape(1, d), w, hgrn_lb, sc_w)


def _in_odd_kernel(x_ref, sh_ref, sc_ref, g_ref, w_ref, wdt_ref, cw_ref, cb_ref, dtb_ref,
                   zs_ref, xs_ref, xdt_ref, bc_ref, dt_ref, mct_ref, tail, *, tm):
    @pl.when(pl.program_id(1) == 0)
    def _():
        tail[...] = jnp.zeros_like(tail)

    h = _modulated_norm(x_ref, sh_ref, sc_ref, g_ref)
    hb = h.astype(BF16)
    dt = _softplus(_dt_proj(h, hb, wdt_ref) + dtb_ref[...])
    dt_ref[0] = dt
    lo = lax.broadcasted_iota(jnp.int32, (tm, LANES), 1) < M_HEADDIM
    for j in range(M_INNER // COL_BLOCK):
        cols = slice(j * COL_BLOCK, (j + 1) * COL_BLOCK)
        zs_ref[0, :, cols] = _silu(jnp.dot(hb, w_ref[:, cols], preferred_element_type=F32))
    n_blocks = M_CONV_DIM // COL_BLOCK
    proj = lambda j: jnp.dot(hb, w_ref[:, M_INNER + j * COL_BLOCK:M_INNER + (j + 1) * COL_BLOCK],
                             preferred_element_type=F32)
    prev_tail = tail[...]
    new_tail = []
    xr_next = proj(0)
    for j in range(n_blocks):
        xr = xr_next
        if j + 1 < n_blocks:
            xr_next = proj(j + 1)
        cols = slice(j * COL_BLOCK, (j + 1) * COL_BLOCK)
        t8 = prev_tail[:, cols]
        cw = cw_ref[:, cols]
        conv = (_shift_rows(xr, t8, 3) * cw[0:1] + _shift_rows(xr, t8, 2) * cw[1:2]
                + _shift_rows(xr, t8, 1) * cw[2:3] + xr * cw[3:4])
        new_tail.append(xr[tm - SUBLANES:tm])
        xbc = _silu(conv + cb_ref[:, cols])
        if j < M_INNER // COL_BLOCK:
            xs_ref[0, :, cols] = xbc
            ppb = COL_BLOCK // LANES
            dt_cols = jnp.concatenate([_pair_cols(dt, lo, 2 * (j * ppb + p)) for p in range(ppb)], axis=1)
            xdt_ref[0, :, cols] = xbc * dt_cols
        else:
            bc_ref[0, :, j * COL_BLOCK - M_INNER:(j + 1) * COL_BLOCK - M_INNER] = xbc.astype(BF16)
    new_tail = jnp.concatenate(new_tail, axis=1)
    tail[...] = new_tail
    mct_ref[0] = new_tail[SUBLANES - (M_CONV - 1):]


def _in_odd(x, mod, g, w, w_dt, conv_w, conv_b, dtb):
    bn, t, d = x.shape
    tm = TOKEN_TILE
    n = w.shape[1]
    tok = lambda width: pl.BlockSpec((1, tm, width), lambda b, i: (b, i, 0))
    return pl.pallas_call(
        functools.partial(_in_odd_kernel, tm=tm),
        grid=(bn, t // tm),
        in_specs=[tok(d), _mod_spec(mod, tm, 0), _mod_spec(mod, tm, 1), _row_spec(d),
                  _const_spec((d, n)), _const_spec((d, 2 * LANES)),
                  pl.BlockSpec((M_CONV, M_CONV_DIM), lambda b, i: (0, 0)),
                  _row_spec(M_CONV_DIM), _row_spec(LANES)],
        out_specs=[tok(M_INNER), tok(M_INNER), tok(M_INNER), tok(2 * M_GN), tok(LANES),
                   pl.BlockSpec((1, M_CONV - 1, M_CONV_DIM), lambda b, i: (b, 0, 0))],
        out_shape=[jax.ShapeDtypeStruct((bn, t, M_INNER), F32),
                   jax.ShapeDtypeStruct((bn, t, M_INNER), F32),
                   jax.ShapeDtypeStruct((bn, t, M_INNER), F32),
                   jax.ShapeDtypeStruct((bn, t, 2 * M_GN), BF16),
                   jax.ShapeDtypeStruct((bn, t, LANES), F32),
                   jax.ShapeDtypeStruct((bn, M_CONV - 1, M_CONV_DIM), F32)],
        scratch_shapes=[pltpu.VMEM((SUBLANES, M_CONV_DIM), F32)],
        compiler_params=_cparams("parallel", "arbitrary"),
        name="in_odd",
    )(x, mod, mod, g.reshape(1, d), w, w_dt, conv_w, conv_b, dtb)


def _post_kernel(*refs, n_o, final, fc):
    x_ref = refs[0]
    o_refs = refs[1:1 + n_o]
    g1_ref, sh2_ref, s2_ref, g2_ref, nm_ref, wo_ref, w1_ref, w2_ref = refs[1 + n_o:9 + n_o]
    rest = refs[9 + n_o:]
    mix, row = None, 0
    for o in o_refs:
        k = o.shape[2]
        part = jnp.dot(o[0].astype(BF16), wo_ref[row:row + k, :], preferred_element_type=F32)
        mix = part if mix is None else mix + part
        row += k
    x1 = x_ref[0] + g1_ref[0] * mix
    h = ((_rms(x1) * nm_ref[...]) * (1.0 + s2_ref[0]) + sh2_ref[0]).astype(BF16)
    acc = jnp.zeros_like(x1)
    for c in range(D_FF // fc):
        a = jnp.dot(h, w1_ref[:, c * fc:(c + 1) * fc], preferred_element_type=F32)
        a = jnp.square(jnp.maximum(a, 0.0)).astype(BF16)
        acc = acc + jnp.dot(a, w2_ref[c * fc:(c + 1) * fc, :], preferred_element_type=F32)
    x2 = x1 + g2_ref[0] * acc
    if final:
        nf_ref, out_ref = rest
        out_ref[0] = _rms(x2) * nf_ref[...]
    else:
        (out_ref,) = rest
        out_ref[0] = x2


def _post(x, os_, mod, nm, wo, w1, w2, layer, nf=None):
    bn, t, d = x.shape
    tm = min(t, TOKEN_TILE)
    final = nf is not None
    layer_spec = lambda r, c: pl.BlockSpec((None, r, c), lambda b, i: (layer, 0, 0), pipeline_mode=pl.Buffered(1))
    in_specs = [pl.BlockSpec((1, tm, d), lambda b, i: (b, i, 0))]
    in_specs += [pl.BlockSpec((1, tm, o.shape[2]), lambda b, i: (b, i, 0)) for o in os_]
    in_specs += [_mod_spec(mod, tm, c) for c in (2, 3, 4, 5)] + [_row_spec(d)]
    in_specs += [_const_spec(wo.shape), layer_spec(d, D_FF), layer_spec(D_FF, d)]
    args = [x, *os_, mod, mod, mod, mod, nm.reshape(1, d), wo, w1, w2]
    if final:
        in_specs.append(_row_spec(d))
        args.append(nf.reshape(1, d))
    return pl.pallas_call(
        functools.partial(_post_kernel, n_o=len(os_), final=final, fc=1024),
        grid=(bn, t // tm),
        in_specs=in_specs,
        out_specs=pl.BlockSpec((1, tm, d), lambda b, i: (b, i, 0)),
        out_shape=jax.ShapeDtypeStruct((bn, t, d), F32),
        compiler_params=_cparams("parallel", "parallel"),
        name="post_final" if final else "post",
    )(*args)


def _gla_kernel(g_ref, gn_ref, o_ref, s_ref, st_scr, *, ts, tb):
    t = pl.program_id(1)

    @pl.when(t == 0)
    def _():
        st_scr[...] = jnp.zeros_like(st_scr)

    mask = _seg_causal(tb, GLA_CHUNK)
    tri = jnp.where(mask, 1.0, 0.0).astype(BF16)
    gn = gn_ref[...]
    n_chunks = tb // GLA_CHUNK
    for sub, h in [(sub, h) for sub in range(ts // tb) for h in range(A_HEADS)]:
        tok = slice(sub * tb, (sub + 1) * tb)
        if h == 0:
            b_all = _dot3(tri, g_ref[0, tok, 2 * A_WIDTH:3 * A_WIDTH])
        part = lambda p: g_ref[0, tok, p * A_WIDTH + h * A_DK:p * A_WIDTH + (h + 1) * A_DK]
        q, k, v, sg = part(0), part(1), part(3), part(4)
        b = b_all[:, h * A_DK:(h + 1) * A_DK]
        qd = q * jnp.exp(b)
        kd = k * jnp.exp(-b)
        o = _bdot(jnp.where(mask, _bdot_nt(qd, kd), 0.0), v)
        kcols, decs = [], []
        for n in range(n_chunks):
            rows = slice(n * GLA_CHUNK, (n + 1) * GLA_CHUNK)
            bl = b[(n + 1) * GLA_CHUNK - 1:(n + 1) * GLA_CHUNK]
            kdec = (k[rows] * jnp.exp(bl - b[rows])).astype(BF16)
            kcols.append(jnp.concatenate(
                ([jnp.zeros((n * GLA_CHUNK, A_DK), BF16)] if n else []) + [kdec]
                + ([jnp.zeros((tb - (n + 1) * GLA_CHUNK, A_DK), BF16)] if n < n_chunks - 1 else []), axis=0))
            decs.append(jnp.exp(bl))
        ds = _bdot(v.T, jnp.concatenate(kcols, axis=1))
        st = st_scr[h]
        inter = []
        for n in range(n_chunks):
            inter.append(_bdot_nt(qd[n * GLA_CHUNK:(n + 1) * GLA_CHUNK], st))
            st = st * decs[n] + ds[:, n * A_DK:(n + 1) * A_DK]
        st_scr[h] = st
        o = o + jnp.concatenate(inter, axis=0)
        o_ref[0, tok, h * A_DK:(h + 1) * A_DK] = ((_rms(o) * gn) * sg).astype(BF16)

    @pl.when(t == pl.num_programs(1) - 1)
    def _():
        for h in range(A_HEADS):
            s_ref[0, h] = st_scr[h].T


def _gla(gla_in, gnorm):
    bn, t, n = gla_in.shape
    ts = GLA_STEP
    return pl.pallas_call(
        functools.partial(_gla_kernel, ts=ts, tb=GLA_TILE),
        grid=(bn, t // ts),
        in_specs=[pl.BlockSpec((1, ts, n), lambda b, i: (b, i, 0)), _row_spec(A_DK)],
        out_specs=[pl.BlockSpec((1, ts, A_WIDTH), lambda b, i: (b, i, 0)),
                   pl.BlockSpec((1, A_HEADS, A_DK, A_DK), lambda b, i: (b, 0, 0, 0))],
        out_shape=[jax.ShapeDtypeStruct((bn, t, A_WIDTH), BF16),
                   jax.ShapeDtypeStruct((bn, A_HEADS, A_DK, A_DK), F32)],
        scratch_shapes=[pltpu.VMEM((A_HEADS, A_DK, A_DK), F32)],
        compiler_params=_cparams("parallel", "arbitrary"),
        name="gla",
    )(gla_in, gnorm.reshape(1, A_DK))


def _even_sample_kernel(proj_ref, s_ref, scb_ref, lb_ref, gn_ref, scw_ref, o_ref, so_ref, sco_ref,
                        *, tt, slot):
    lb = _hgrn_lower_bound(lb_ref[...], slot)
    gn = gn_ref[...]
    for h in range(A_HEADS):
        cols = lambda part: slice(part * A_WIDTH + h * A_DK, part * A_WIDTH + (h + 1) * A_DK)
        q = proj_ref[:, cols(0)]
        fz = proj_ref[:, cols(1)]
        v = proj_ref[:, cols(2)]
        go = proj_ref[:, cols(3)]
        lbh = lb[:, h * A_DK:(h + 1) * A_DK]
        f = lbh + (1.0 - lbh) * jax.nn.sigmoid(fz)
        f_t = _pad_t(f)
        q_t = _pad_t(q)
        outs = []
        for i in range(tt):
            fc = f_t[:, i:i + 1]
            s_new = fc * s_ref[i, h] + (1.0 - fc) * v[i:i + 1, :]
            so_ref[i, h] = s_new
            outs.append(jnp.sum(q_t[:, i:i + 1] * s_new, axis=0, keepdims=True))
        o = jnp.concatenate(outs, axis=0)
        o_ref[:, h * A_DK:(h + 1) * A_DK] = (_rms(o) * gn) * _silu(go)

    off = 4 * A_WIDTH
    bg = proj_ref[:, off:off + B_WIDTH]
    u = proj_ref[:, off + B_WIDTH:off + 2 * B_WIDTH] * proj_ref[:, off + 2 * B_WIDTH:off + 3 * B_WIDTH]
    w = scw_ref[...]
    conv = scb_ref[:, 0:B_WIDTH] * w[0:1] + scb_ref[:, B_WIDTH:2 * B_WIDTH] * w[1:2] + u * w[2:3]
    o_ref[:, A_WIDTH:A_WIDTH + B_WIDTH] = bg * conv
    sco_ref[:, 0:B_WIDTH] = scb_ref[:, B_WIDTH:2 * B_WIDTH]
    sco_ref[:, B_WIDTH:2 * B_WIDTH] = u


def _even_sample(proj, state, sc_buf, hgrn_lb, gnorm, sc_w, slot):
    bs, n = proj.shape
    tt = SUBLANES
    nb = sc_buf.shape[1]
    return pl.pallas_call(
        functools.partial(_even_sample_kernel, tt=tt, slot=slot),
        grid=(bs // tt,),
        in_specs=[pl.BlockSpec((tt, n), lambda i: (i, 0)),
                  pl.BlockSpec((tt, A_HEADS, A_DK, A_DK), lambda i: (i, 0, 0, 0)),
                  pl.BlockSpec((tt, nb), lambda i: (i, 0)),
                  pl.BlockSpec(hgrn_lb.shape, lambda i: (0, 0)),
                  pl.BlockSpec((1, A_DK), lambda i: (0, 0)),
                  pl.BlockSpec((SC_WIDTH, B_WIDTH), lambda i: (0, 0))],
        out_specs=[pl.BlockSpec((tt, D_MODEL), lambda i: (i, 0)),
                   pl.BlockSpec((tt, A_HEADS, A_DK, A_DK), lambda i: (i, 0, 0, 0)),
                   pl.BlockSpec((tt, nb), lambda i: (i, 0))],
        out_shape=[jax.ShapeDtypeStruct((bs, D_MODEL), F32),
                   jax.ShapeDtypeStruct(state.shape, F32),
                   jax.ShapeDtypeStruct(sc_buf.shape, F32)],
        compiler_params=_cparams("parallel"),
        name="even_sample",
    )(proj, state, sc_buf, hgrn_lb, gnorm.reshape(1, A_DK), sc_w)


def _pair_cols(m, lo, h0):
    return jnp.where(lo, m[:, h0:h0 + 1], m[:, h0 + 1:h0 + 2])


def _ssd_kernel(xs_ref, xdt_ref, bc_ref, zs_ref, dt_ref, alog_ref, dsk_ref, nw_ref, y_ref, s_ref, st_scr,
                *, tq):
    t = pl.program_id(1)
    q = SSD_CHUNK

    @pl.when(t == 0)
    def _():
        st_scr[...] = jnp.zeros_like(st_scr)

    r = lax.broadcasted_iota(jnp.int32, (q, q), 0)
    c = lax.broadcasted_iota(jnp.int32, (q, q), 1)
    causal = c <= r
    tri = jnp.where(causal, 1.0, 0.0).astype(BF16)
    lo = c < M_HEADDIM
    lo1 = lo[0:1]
    neg_a = -jnp.exp(alog_ref[...])
    for ck in range(tq // q):
        _ssd_chunk(slice(ck * q, (ck + 1) * q), tri, causal, lo, lo1, neg_a,
                   xs_ref, xdt_ref, bc_ref, zs_ref, dt_ref, dsk_ref, nw_ref, y_ref, st_scr)

    @pl.when(t == pl.num_programs(1) - 1)
    def _():
        for m in range(M_INNER // LANES):
            blk = st_scr[:, m * LANES:(m + 1) * LANES].T
            s_ref[0, 2 * m] = blk[0:M_HEADDIM]
            s_ref[0, 2 * m + 1] = blk[M_HEADDIM:2 * M_HEADDIM]


def _ssd_chunk(rows, tri, causal, lo, lo1, neg_a,
               xs_ref, xdt_ref, bc_ref, zs_ref, dt_ref, dsk_ref, nw_ref, y_ref, st_scr):
    q = SSD_CHUNK
    d_a = dt_ref[0, rows, :] * neg_a
    cs = _dot3(tri, d_a)
    cs_t = cs.T
    cs_last = cs[q - 1:q, :]
    dec = jnp.exp(cs_last)
    gw = M_INNER // M_GROUPS
    for g in range(M_GROUPS):
        bm = bc_ref[0, rows, g * M_STATE:(g + 1) * M_STATE]
        cm = bc_ref[0, rows, M_GN + g * M_STATE:M_GN + (g + 1) * M_STATE]
        cb = _bdot_nt(cm, bm)
        st_g = st_scr[:, g * gw:(g + 1) * gw]
        y_off = _bdot(cm, st_g)
        ys, xws, decs = [], [], []
        for p in range(gw // LANES):
            m = g * (gw // LANES) + p
            h0 = 2 * m
            xp = xs_ref[0, rows, m * LANES:(m + 1) * LANES]
            xdp = xdt_ref[0, rows, m * LANES:(m + 1) * LANES]
            yd, e_in, e_out = [], [], []
            for h in (h0, h0 + 1):
                col = jnp.broadcast_to(cs[:, h:h + 1], (q, q))
                seg = jnp.minimum(col - cs_t[h:h + 1, :], 0.0)
                yd.append(_bdot(cb * jnp.where(causal, jnp.exp(seg), 0.0), xdp))
                e_in.append(jnp.exp(col))
                e_out.append(jnp.exp(cs_last[:, h:h + 1] - col))
            y = jnp.where(lo, yd[0], yd[1])
            y = y + y_off[:, p * LANES:(p + 1) * LANES] * jnp.where(lo, e_in[0], e_in[1])
            ys.append(y + dsk_ref[:, m * LANES:(m + 1) * LANES] * xp)
            xws.append(xdp * jnp.where(lo, e_out[0], e_out[1]))
            decs.append(_pair_cols(dec, lo1, h0))
        xw = jnp.concatenate(xws, axis=1)
        st_scr[:, g * gw:(g + 1) * gw] = (st_g * jnp.concatenate(decs, axis=1)
                                          + _bdot(bm.astype(F32).T, xw))
        y = jnp.concatenate(ys, axis=1) * zs_ref[0, rows, g * gw:(g + 1) * gw]
        y_ref[0, rows, g * gw:(g + 1) * gw] = (_rms(y) * nw_ref[:, g * gw:(g + 1) * gw]).astype(BF16)


def _ssd(xs, xdt, bc, zs, dt, alog, dsk, nw):
    bn, t, _ = xs.shape
    tq = SSD_TILE
    tok = lambda width: pl.BlockSpec((1, tq, width), lambda b, i: (b, i, 0))
    return pl.pallas_call(
        functools.partial(_ssd_kernel, tq=tq),
        grid=(bn, t // tq),
        in_specs=[tok(M_INNER), tok(M_INNER), tok(2 * M_GN), tok(M_INNER), tok(LANES),
                  _row_spec(LANES), _row_spec(M_INNER), _row_spec(M_INNER)],
        out_specs=[tok(M_INNER),
                   pl.BlockSpec((1, M_HEADS, M_HEADDIM, M_STATE), lambda b, i: (b, 0, 0, 0))],
        out_shape=[jax.ShapeDtypeStruct((bn, t, M_INNER), BF16),
                   jax.ShapeDtypeStruct((bn, M_HEADS, M_HEADDIM, M_STATE), F32)],
        scratch_shapes=[pltpu.VMEM((M_STATE, M_INNER), F32)],
        compiler_params=_cparams("parallel", "arbitrary"),
        name="ssd",
    )(xs, xdt, bc, zs, dt, alog, dsk, nw)


def _mconv_sample_kernel(zx_ref, dtr_ref, buf_ref, cw_ref, cb_ref, dtb_ref, alog_ref,
                         xbc_ref, nbuf_ref, dte_ref, decb_ref):
    n = M_CONV_DIM
    u = zx_ref[:, M_INNER:M_INNER + n]
    cw = cw_ref[...]
    conv = (buf_ref[:, 0:n] * cw[0:1] + buf_ref[:, n:2 * n] * cw[1:2]
            + buf_ref[:, 2 * n:3 * n] * cw[2:3] + u * cw[3:4])
    xbc_ref[...] = _silu(conv + cb_ref[...])
    nbuf_ref[:, 0:2 * n] = buf_ref[:, n:3 * n]
    nbuf_ref[:, 2 * n:3 * n] = u
    dt = _softplus(dtr_ref[...] + dtb_ref[...])
    dec = jnp.exp(dt * (-jnp.exp(alog_ref[...])))
    tt = dt.shape[0]
    lo = lax.broadcasted_iota(jnp.int32, (tt, LANES), 1) < M_HEADDIM
    for m in range(M_INNER // LANES):
        dte_ref[:, m * LANES:(m + 1) * LANES] = _pair_cols(dt, lo, 2 * m)
    for h in range(M_HEADS):
        decb_ref[h] = jnp.broadcast_to(dec[:, h:h + 1], (tt, LANES))


def _mconv_sample(zx, dtr, buf, conv_w, conv_b, dtb, alog):
    bs, n = zx.shape
    tt = 32
    nb = buf.shape[1]
    row = lambda w: pl.BlockSpec((1, w), lambda i: (0, 0))
    tok = lambda w: pl.BlockSpec((tt, w), lambda i: (i, 0))
    return pl.pallas_call(
        _mconv_sample_kernel,
        grid=(bs // tt,),
        in_specs=[tok(n), tok(LANES), tok(nb),
                  pl.BlockSpec((M_CONV, M_CONV_DIM), lambda i: (0, 0)),
                  row(M_CONV_DIM), row(LANES), row(LANES)],
        out_specs=[tok(M_CONV_DIM), tok(nb), tok(M_INNER),
                   pl.BlockSpec((M_HEADS, tt, LANES), lambda i: (0, i, 0))],
        out_shape=[jax.ShapeDtypeStruct((bs, M_CONV_DIM), F32),
                   jax.ShapeDtypeStruct((bs, nb), F32),
                   jax.ShapeDtypeStruct((bs, M_INNER), F32),
                   jax.ShapeDtypeStruct((M_HEADS, bs, LANES), F32)],
        compiler_params=_cparams("parallel"),
        name="mconv_sample",
    )(zx, dtr, buf, conv_w, conv_b, dtb, alog)


def _ssd_sample_kernel(x_ref, b_ref, c_ref, z_ref, dte_ref, decb_ref, s_ref, dsk_ref, nw_ref,
                       y_ref, so_ref, *, tt):
    x = x_ref[...]
    gw = x.shape[1]
    hpg = gw // M_HEADDIM
    xh, xm, xl = _split3(x * dte_ref[...])
    bh, bmid, bl = _split3(b_ref[...])
    n_terms = 6
    lhs_t = jnp.concatenate([xh, xh, xm, xh, xm, xl, jnp.zeros((LANES - n_terms * tt, gw), F32)], axis=0)
    lhs = jnp.concatenate([lhs_t[:, p * LANES:(p + 1) * LANES].T for p in range(gw // LANES)],
                          axis=0).astype(BF16)
    rhs_all = jnp.concatenate([bh, bmid, bh, bl, bmid, bh,
                               jnp.zeros((LANES - n_terms * tt, M_STATE), F32)], axis=0)
    row_tok = lax.broadcasted_iota(jnp.int32, (LANES, M_STATE), 0) & (tt - 1)
    cmb = c_ref[...].astype(BF16)
    out_row = lax.broadcasted_iota(jnp.int32, (tt, gw), 0)
    y = jnp.zeros((tt, gw), F32)
    for i in range(tt):
        rhs = jnp.where(row_tok == i, rhs_all, 0.0).astype(BF16)
        xb = jnp.dot(lhs, rhs, preferred_element_type=F32)
        pieces = []
        for hh in range(hpg):
            rows = slice(hh * M_HEADDIM, (hh + 1) * M_HEADDIM)
            s_new = decb_ref[hh, i:i + 1, :] * s_ref[i, hh] + xb[rows]
            so_ref[i, hh] = s_new
            pieces.append(s_new.astype(BF16))
        yi = lax.dot_general(cmb, jnp.concatenate(pieces, axis=0), (((1,), (1,)), ((), ())),
                             preferred_element_type=F32)
        y = jnp.where(out_row == i, yi, y)
    y = (y + dsk_ref[...] * x) * _silu(z_ref[...])
    y_ref[...] = _rms(y) * nw_ref[...]


def _ssd_sample(xbc, zx, dte, decb, state, dsk, nw):
    bs = xbc.shape[0]
    tt = SUBLANES
    gw = M_INNER // M_GROUPS
    hpg = M_HEADS // M_GROUPS
    return pl.pallas_call(
        functools.partial(_ssd_sample_kernel, tt=tt),
        grid=(bs // tt, M_GROUPS),
        in_specs=[pl.BlockSpec((tt, gw), lambda i, g: (i, g)),
                  pl.BlockSpec((tt, M_STATE), lambda i, g: (i, M_INNER // M_STATE + g)),
                  pl.BlockSpec((tt, M_STATE), lambda i, g: (i, (M_INNER + M_GN) // M_STATE + g)),
                  pl.BlockSpec((tt, gw), lambda i, g: (i, g)),
                  pl.BlockSpec((tt, gw), lambda i, g: (i, g)),
                  pl.BlockSpec((hpg, tt, LANES), lambda i, g: (g, i, 0)),
                  pl.BlockSpec((tt, hpg, M_HEADDIM, M_STATE), lambda i, g: (i, g, 0, 0)),
                  pl.BlockSpec((1, gw), lambda i, g: (0, g)),
                  pl.BlockSpec((1, gw), lambda i, g: (0, g))],
        out_specs=[pl.BlockSpec((tt, gw), lambda i, g: (i, g)),
                   pl.BlockSpec((tt, hpg, M_HEADDIM, M_STATE), lambda i, g: (i, g, 0, 0))],
        out_shape=[jax.ShapeDtypeStruct((bs, M_INNER), F32),
                   jax.ShapeDtypeStruct(state.shape, F32)],
        compiler_params=_cparams("parallel", "parallel"),
        name="ssd_sample",
    )(xbc, xbc, xbc, zx, dte, decb, state, dsk, nw)


def _pad_lanes(v):
    return jnp.pad(v.reshape(1, -1), ((0, 0), (0, LANES - v.shape[-1])))


def kernel(x_prompt, x_sample, c_prompt, c_sample, state_hgrn, state_shortconv, state_ssm, state_mconv, ada_w, ada_b, norm_mix, norm_mlp, norm_final, w_in_even, hgrn_lb, hgrn_gnorm, sc_w, w_out_even, w_in_odd, mconv_w, mconv_b, dt_bias, a_log, d_skip, m_norm, w_out_odd, mlp_w1, mlp_w2):
    bp = x_prompt.shape[0]
    bs = x_sample.shape[0]
    mod = _adaln_mod(jnp.concatenate([c_prompt, c_sample], axis=0), ada_w, ada_b)
    mod_p = [mod[l, :bp].reshape(bp, 1, -1) for l in range(2)]
    mod_s = [mod[l, bp:].reshape(1, bs, -1) for l in range(2)]

    w_even = w_in_even[0].astype(BF16)
    wo_even = w_out_even[0].astype(BF16)
    nzx = M_INNER + M_CONV_DIM
    w_odd = w_in_odd[0].astype(BF16)
    w_dt = jnp.pad(w_in_odd[0][:, nzx:], ((0, 0), (0, LANES - M_HEADS)))
    w_dt_hi = w_dt.astype(BF16)
    w_dt = jnp.concatenate([w_dt_hi, (w_dt - w_dt_hi.astype(F32)).astype(BF16)], axis=1)
    wo_odd = w_out_odd[0].astype(BF16)
    w1 = mlp_w1.astype(BF16)
    w2 = mlp_w2.astype(BF16)
    dtb = _pad_lanes(dt_bias[0])
    alog = _pad_lanes(a_log[0])
    dsk = jnp.repeat(d_skip[0], M_HEADDIM).reshape(1, M_INNER)
    nw = m_norm[0].reshape(1, M_INNER)
    cb = mconv_b[0].reshape(1, M_CONV_DIM)

    gla_in, o_b, sc_p = _in_even(x_prompt, mod_p[0], norm_mix[0], w_even, hgrn_lb, sc_w[0], 1)
    o_a, hg_p = _gla(gla_in, hgrn_gnorm[0])
    x1 = _post(x_prompt, [o_a, o_b], mod_p[0], norm_mlp[0], wo_even, w1, w2, 0)
    zs, xs_c, xdt, bc, dt, mc_p = _in_odd(x1, mod_p[1], norm_mix[1], w_odd, w_dt, mconv_w[0], cb, dtb)
    y, ssm_p = _ssd(xs_c, xdt, bc, zs, dt, alog, dsk, nw)
    y_prompt = _post(x1, [y], mod_p[1], norm_mlp[1], wo_odd, w1, w2, 1, norm_final)

    xs = x_sample.reshape(1, bs, D_MODEL)
    (proj_s,) = _norm_proj(xs, mod_s[0], norm_mix[0], w_even, w_even.shape[1])
    o_s, hg_s, sc_s = _even_sample(proj_s[0], state_hgrn[0], state_shortconv[0].reshape(bs, -1),
                                   hgrn_lb, hgrn_gnorm[0], sc_w[0], 1)
    x1s = _post(xs, [o_s[None]], mod_s[0], norm_mlp[0], wo_even, w1, w2, 0)
    zx_s, dtr_s = _norm_proj(x1s, mod_s[1], norm_mix[1], w_odd, nzx, w_dt)
    xbc_s, mc_s, dte, decb = _mconv_sample(zx_s[0], dtr_s[0], state_mconv[0].reshape(bs, -1),
                                           mconv_w[0], cb, dtb, alog)
    y_s, ssm_s = _ssd_sample(xbc_s, zx_s[0], dte, decb, state_ssm[0], dsk, nw)
    y_sample = _post(x1s, [y_s[None]], mod_s[1], norm_mlp[1], wo_odd, w1, w2, 1, norm_final)

    return (y_prompt, y_sample.reshape(bs, 1, D_MODEL),
            hg_p[None], hg_s[None],
            sc_p[None], sc_s.reshape(1, bs, SC_WIDTH - 1, B_WIDTH),
            ssm_p[None], ssm_s[None],
            mc_p[None], mc_s.reshape(1, bs, M_CONV - 1, M_CONV_DIM))
```
